```python
import jax, jax.numpy as jnp
from jax import lax
import numpy as np

D_MODEL = 4096
BATCH = 4
SEQ = 2048
DEPTH = 4
DEC_BATCH = 128
DEC_SEQ = 1
PAST_LEN = 8192
PAGE_SIZE = 128

MLA_HEADS = 16
MLA_NOPE = 128
MLA_ROPE = 64
MLA_V = 128
Q_RANK = 768
KV_RANK = 256
MLA_SCALE = (MLA_NOPE + MLA_ROPE) ** -0.5
NSA_HEADS = 32
NSA_DK = 64
NSA_ROT = NSA_DK // 4
NSA_SCALE = NSA_DK ** -0.5
CMP_LEN = 32
CMP_STRIDE = 16
CMP_HIDDEN = 256
SLC_BLOCK = 64
N_SELECT = 16
WINDOW = 512
D_FF = 4 * D_MODEL
ROPE_THETA = 500000.0
EPS = 1e-6
QBLK = 128
NEG = -1e30
FORCE_SCORE = 1e9
POOL_NUM = 5
POOL_DEN = 4
C_IN = Q_RANK + KV_RANK + MLA_ROPE + NSA_HEADS * NSA_DK + 6 * NSA_DK + 3 * NSA_HEADS + 2 * D_MODEL

kernel_name = 'hybrid_mla_nsa_decoder_step'


def rms_norm(x, g):
    xf = x.astype(jnp.float32)
    y = xf * lax.rsqrt(jnp.mean(xf * xf, axis=-1, keepdims=True) + EPS)
    return (y * g.astype(jnp.float32)).astype(x.dtype)


def apply_rope(x, pos, rot_dim):
    half = rot_dim // 2
    inv = ROPE_THETA ** (-jnp.arange(half, dtype=jnp.float32) / half)
    ang = pos.astype(jnp.float32)[:, None] * inv[None, :]
    shape = (ang.shape[0],) + (1,) * (x.ndim - 3) + (half,)
    cos = jnp.cos(ang).reshape(shape).astype(x.dtype)
    sin = jnp.sin(ang).reshape(shape).astype(x.dtype)
    x1, x2, rest = x[..., :half], x[..., half:rot_dim], x[..., rot_dim:]
    return jnp.concatenate([x1 * cos - x2 * sin, x2 * cos + x1 * sin, rest], axis=-1)


def masked_softmax(s, mask):
    s = jnp.where(mask, s.astype(jnp.float32), NEG)
    m = jnp.max(s, axis=-1, keepdims=True)
    e = jnp.where(mask, jnp.exp(s - m), 0.0)
    return e / jnp.maximum(jnp.sum(e, axis=-1, keepdims=True), 1e-30)


def gather_pages(pool, layer, page_table):
    rows = pool[layer, page_table]
    return rows.reshape(page_table.shape[0], -1, pool.shape[-1])


def sweep_query_blocks(fn, t):
    starts = jnp.arange(t // QBLK, dtype=jnp.int32) * QBLK
    out = lax.map(fn, starts)
    nb, b, qb, f = out.shape
    return jnp.transpose(out, (1, 0, 2, 3)).reshape(b, nb * qb, f)


def project_inputs(h, pos, w_in_l, g_qa_l, w_qb_l, g_kva_l, g_qn_l, g_qr_l, g_kr_l, g_nq_l, g_nks_l, g_nkw_l):
    b, t, _ = h.shape
    z = jnp.einsum('btd,dc->btc', h, w_in_l)
    widths = [Q_RANK, KV_RANK, MLA_ROPE, NSA_HEADS * NSA_DK] + [NSA_DK] * 6 + [3 * NSA_HEADS, D_MODEL, D_MODEL]
    off = np.cumsum([0] + widths)
    parts = [z[..., int(off[i]):int(off[i + 1])] for i in range(len(widths))]
    q_lat, kv_lat, kr, nq, kc_raw, vc_raw, ks, vs, kw, vw, ngate, gate_a, gate_b = parts
    q = jnp.einsum('btr,rc->btc', rms_norm(q_lat, g_qa_l), w_qb_l).reshape(b, t, MLA_HEADS, MLA_NOPE + MLA_ROPE)
    q_nope = rms_norm(q[..., :MLA_NOPE], g_qn_l)
    q_rope = apply_rope(rms_norm(q[..., MLA_NOPE:], g_qr_l), pos, MLA_ROPE)
    c = rms_norm(kv_lat, g_kva_l)
    k_rope = apply_rope(rms_norm(kr, g_kr_l), pos, MLA_ROPE)
    nq = apply_rope(rms_norm(nq.reshape(b, t, NSA_HEADS, NSA_DK), g_nq_l), pos, NSA_ROT)
    ks = apply_rope(rms_norm(ks, g_nks_l), pos, NSA_ROT)
    kw = apply_rope(rms_norm(kw, g_nkw_l), pos, NSA_ROT)
    ngate = ngate.reshape(b, t, NSA_HEADS, 3)
    return (q_nope, q_rope, c, k_rope, nq, kc_raw, vc_raw, ks, vs, kw, vw, ngate, gate_a, gate_b)


def mla_key_scale(c, w_kb):
    def per_head(w_h):
        k = jnp.einsum('btr,rd->btd', c, w_h).astype(jnp.float32)
        return lax.rsqrt(jnp.mean(k * k, axis=-1) + EPS)
    return jnp.transpose(lax.map(per_head, w_kb), (1, 0, 2))


def mla_attend(q_nope, q_rope, qpos, c, k_rope, r_k, kpos, w_kb, w_vb, g_kn):
    b, tq, h, _ = q_nope.shape
    q_abs = jnp.einsum('bqhd,hrd->bqhr', q_nope * g_kn, w_kb)
    s = jnp.einsum('bqhr,bkr->bhqk', q_abs, c) * r_k[:, :, None, :].astype(c.dtype)
    s = (s + jnp.einsum('bqhd,bkd->bhqk', q_rope, k_rope)) * MLA_SCALE
    p = masked_softmax(s, kpos[None, :] <= qpos[:, None])
    o_lat = jnp.einsum('bhqk,bkr->bqhr', p.astype(c.dtype), c)
    return jnp.einsum('bqhr,hrv->bqhv', o_lat, w_vb).reshape(b, tq, h * MLA_V)


def compress_rows(rows, pos_emb, w1, b1, w2):
    b, t, dk = rows.shape
    nc = (t - CMP_LEN) // CMP_STRIDE + 1
    idx = jnp.arange(nc)[:, None] * CMP_STRIDE + jnp.arange(CMP_LEN)[None, :]
    blocks = (rows[:, idx] + pos_emb).reshape(b, nc, CMP_LEN * dk)
    return jax.nn.gelu(blocks @ w1 + b1) @ w2


def compressed_branch(k_raw, v_raw, g_nkc_l, cmp_pos_l, cmp_w1_l, cmp_b1_l, cmp_w2_l):
    kc = compress_rows(k_raw, cmp_pos_l[0], cmp_w1_l[0], cmp_b1_l[0], cmp_w2_l[0])
    vc = compress_rows(v_raw, cmp_pos_l[1], cmp_w1_l[1], cmp_b1_l[1], cmp_w2_l[1])
    cend = jnp.arange(kc.shape[1]) * CMP_STRIDE + CMP_LEN - 1
    kc = apply_rope(rms_norm(kc, g_nkc_l), cend, NSA_ROT)
    return kc, vc, cend


def to_blocks(rows):
    b, t, d = rows.shape
    ns = -(-t // SLC_BLOCK)
    rows = jnp.pad(rows, ((0, 0), (0, ns * SLC_BLOCK - t), (0, 0)))
    return rows.reshape(b, ns, SLC_BLOCK, d)


def nsa_attend(q, qpos, kc, vc, cend, ks_blk, vs_blk, kw, vw, wpos, gate_logits):
    b, tq, h, dk = q.shape
    s_c = jnp.einsum('bqhd,bnd->bhqn', q, kc) * NSA_SCALE
    p_c = masked_softmax(s_c, cend[None, :] <= qpos[:, None])
    o_c = jnp.einsum('bhqn,bnd->bqhd', p_c.astype(vc.dtype), vc)
    ns = ks_blk.shape[1]
    blk = jnp.arange(ns)
    bstart = blk * SLC_BLOCK
    cstart = cend - (CMP_LEN - 1)
    overlap = ((cstart[:, None] < bstart[None, :] + SLC_BLOCK) & (cend[:, None] >= bstart[None, :])).astype(jnp.float32)
    imp = jnp.einsum('bhqn,ns->bqs', p_c, overlap)
    cur = qpos // SLC_BLOCK
    forced = (blk[None, :] == 0) | (blk[None, :] == cur[:, None]) | (blk[None, :] == cur[:, None] - 1)
    valid = bstart[None, :] <= qpos[:, None]
    imp = jnp.where(forced[None], FORCE_SCORE, jnp.where(valid[None], imp, -FORCE_SCORE))
    _, sel = lax.top_k(imp, min(N_SELECT, ns))
    bidx = jnp.arange(b)[:, None, None]
    ksel = ks_blk[bidx, sel].reshape(b, tq, -1, dk)
    vsel = vs_blk[bidx, sel].reshape(b, tq, -1, dk)
    spos = (sel[..., None] * SLC_BLOCK + jnp.arange(SLC_BLOCK)).reshape(b, tq, -1)
    s_s = jnp.einsum('bqhd,bqmd->bhqm', q, ksel) * NSA_SCALE
    p_s = masked_softmax(s_s, (spos <= qpos[None, :, None])[:, None])
    o_s = jnp.einsum('bhqm,bqmd->bqhd', p_s.astype(vsel.dtype), vsel)
    s_w = jnp.einsum('bqhd,bkd->bhqk', q, kw) * NSA_SCALE
    dist = qpos[:, None] - wpos[None, :]
    p_w = masked_softmax(s_w, (dist >= 0) & (dist < WINDOW) & (wpos[None, :] >= 0))
    o_w = jnp.einsum('bhqk,bkd->bqhd', p_w.astype(vw.dtype), vw)
    g = jax.nn.sigmoid(gate_logits.astype(jnp.float32)).astype(q.dtype)
    o = g[..., 0:1] * o_c + g[..., 1:2] * o_s + g[..., 2:3] * o_w
    return o.reshape(b, tq, h * dk)


def merge_and_mlp(x, o_mla, o_nsa, gate_a, gate_b, w_mla_o_l, w_nsa_o_l, w_out_l, g_mlp_l, w_up_l, w_down_l):
    y = jax.nn.sigmoid(gate_a) * (o_mla @ w_mla_o_l) + jax.nn.sigmoid(gate_b) * (o_nsa @ w_nsa_o_l)
    x = x + y @ w_out_l
    h = rms_norm(x, g_mlp_l)
    return x + jnp.square(jax.nn.relu(h @ w_up_l)) @ w_down_l


def setup_inputs(seed: int = 0) -> dict:
    key = jax.random.key(seed)
    ks = jax.random.split(key, 40)
    n_pages = PAST_LEN // PAGE_SIZE
    n_pool = (DEC_BATCH * n_pages * POOL_NUM) // POOL_DEN
    wbuf = min(WINDOW, PAST_LEN)
    f32 = jnp.float32

    def nrm(k, shape, scale=1.0):
        return jax.random.normal(k, shape, f32) * scale

    def gain(k, shape):
        return 1.0 + 0.05 * jax.random.normal(k, shape, f32)

    page_table = jax.random.permutation(ks[10], n_pool)[: DEC_BATCH * n_pages].reshape(DEC_BATCH, n_pages).astype(jnp.int32)
    return {
        'x_prompt': nrm(ks[0], (BATCH, SEQ, D_MODEL)),
        'x_sample': nrm(ks[1], (DEC_BATCH, DEC_SEQ, D_MODEL)),
        'cache_mla_latent': nrm(ks[2], (DEPTH, n_pool, PAGE_SIZE, KV_RANK)),
        'cache_mla_krope': nrm(ks[3], (DEPTH, n_pool, PAGE_SIZE, MLA_ROPE)),
        'cache_nsa_kcmp': nrm(ks[4], (DEPTH, n_pool, PAGE_SIZE, NSA_DK)),
        'cache_nsa_vcmp': nrm(ks[5], (DEPTH, n_pool, PAGE_SIZE, NSA_DK)),
        'cache_nsa_kslc': nrm(ks[6], (DEPTH, n_pool, PAGE_SIZE, NSA_DK)),
        'cache_nsa_vslc': nrm(ks[7], (DEPTH, n_pool, PAGE_SIZE, NSA_DK)),
        'state_nsa_kwin': nrm(ks[8], (DEPTH, DEC_BATCH, wbuf, NSA_DK)),
        'state_nsa_vwin': nrm(ks[9], (DEPTH, DEC_BATCH, wbuf, NSA_DK)),
        'page_table': page_table,
        'g_attn': gain(ks[11], (DEPTH, D_MODEL)),
        'w_in': nrm(ks[12], (DEPTH, D_MODEL, C_IN), D_MODEL ** -0.5),
        'g_qa': gain(ks[13], (DEPTH, Q_RANK)),
        'w_qb': nrm(ks[14], (DEPTH, Q_RANK, MLA_HEADS * (MLA_NOPE + MLA_ROPE)), Q_RANK ** -0.5),
        'g_kva': gain(ks[15], (DEPTH, KV_RANK)),
        'g_qn': gain(ks[16], (DEPTH, MLA_NOPE)),
        'g_qr': gain(ks[17], (DEPTH, MLA_ROPE)),
        'g_kn': gain(ks[18], (DEPTH, MLA_NOPE)),
        'g_kr': gain(ks[19], (DEPTH, MLA_ROPE)),
        'w_kb': nrm(ks[20], (DEPTH, MLA_HEADS, KV_RANK, MLA_NOPE), KV_RANK ** -0.5),
        'w_vb': nrm(ks[21], (DEPTH, MLA_HEADS, KV_RANK, MLA_V), KV_RANK ** -0.5),
        'w_mla_o': nrm(ks[22], (DEPTH, MLA_HEADS * MLA_V, D_MODEL), (MLA_HEADS * MLA_V) ** -0.5),
        'g_nq': gain(ks[23], (DEPTH, NSA_DK)),
        'g_nkc': gain(ks[24], (DEPTH, NSA_DK)),
        'g_nks': gain(ks[25], (DEPTH, NSA_DK)),
        'g_nkw': gain(ks[26], (DEPTH, NSA_DK)),
        'cmp_pos': nrm(ks[27], (DEPTH, 2, CMP_LEN, NSA_DK), 0.5),
        'cmp_w1': nrm(ks[28], (DEPTH, 2, CMP_LEN * NSA_DK, CMP_HIDDEN), (CMP_LEN * NSA_DK) ** -0.5),
        'cmp_b1': nrm(ks[29], (DEPTH, 2, CMP_HIDDEN), 0.02),
        'cmp_w2': nrm(ks[30], (DEPTH, 2, CMP_HIDDEN, NSA_DK), CMP_HIDDEN ** -0.5),
        'w_nsa_o': nrm(ks[31], (DEPTH, NSA_HEADS * NSA_DK, D_MODEL), (NSA_HEADS * NSA_DK) ** -0.5),
        'w_out': nrm(ks[32], (DEPTH, D_MODEL, D_MODEL), D_MODEL ** -0.5),
        'g_mlp': gain(ks[33], (DEPTH, D_MODEL)),
        'w_up': nrm(ks[34], (DEPTH, D_MODEL, D_FF), D_MODEL ** -0.5),
        'w_down': nrm(ks[35], (DEPTH, D_FF, D_MODEL), D_FF ** -0.5),
    }


def reference(x_prompt, x_sample, cache_mla_latent, cache_mla_krope, cache_nsa_kcmp, cache_nsa_vcmp,
              cache_nsa_kslc, cache_nsa_vslc, state_nsa_kwin, state_nsa_vwin, page_table,
              g_attn, w_in, g_qa, w_qb, g_kva, g_qn, g_qr, g_kn, g_kr, w_kb, w_vb, w_mla_o,
              g_nq, g_nkc, g_nks, g_nkw, cmp_pos, cmp_w1, cmp_b1, cmp_w2, w_nsa_o,
              w_out, g_mlp, w_up, w_down):
    t_p = x_prompt.shape[1]
    t_s = x_sample.shape[1]
    wbuf = state_nsa_kwin.shape[2]
    wkeep_p = min(WINDOW, t_p)
    pos_p = jnp.arange(t_p, dtype=jnp.int32)
    pos_s = PAST_LEN + jnp.arange(t_s, dtype=jnp.int32)
    pos_all = jnp.arange(PAST_LEN + t_s, dtype=jnp.int32)
    wpos_s = PAST_LEN - wbuf + jnp.arange(wbuf + t_s, dtype=jnp.int32)
    xp, xs = x_prompt, x_sample
    p_rows, s_rows = [], []
    for l in range(DEPTH):
        lw = (w_in[l], g_qa[l], w_qb[l], g_kva[l], g_qn[l], g_qr[l], g_kr[l], g_nq[l], g_nks[l], g_nkw[l])
        cmp_l = (g_nkc[l], cmp_pos[l], cmp_w1[l], cmp_b1[l], cmp_w2[l])
        out_l = (w_mla_o[l], w_nsa_o[l], w_out[l], g_mlp[l], w_up[l], w_down[l])

        hp = rms_norm(xp, g_attn[l])
        (qn_p, qr_p, c_p, kr_p, nq_p, kcr_p, vcr_p, ks_p, vs_p, kw_p, vw_p, ng_p, ga_p, gb_p) = project_inputs(hp, pos_p, *lw)
        r_p = mla_key_scale(c_p, w_kb[l])

        def mla_block(qs):
            q1 = lax.dynamic_slice_in_dim(qn_p, qs, QBLK, axis=1)
            q2 = lax.dynamic_slice_in_dim(qr_p, qs, QBLK, axis=1)
            return mla_attend(q1, q2, qs + jnp.arange(QBLK, dtype=jnp.int32), c_p, kr_p, r_p, pos_p, w_kb[l], w_vb[l], g_kn[l])
        o_mla_p = sweep_query_blocks(mla_block, t_p)

        kc_p, vc_p, cend_p = compressed_branch(kcr_p, vcr_p, *cmp_l)
        ksb_p, vsb_p = to_blocks(ks_p), to_blocks(vs_p)
        kw_pad = jnp.pad(kw_p, ((0, 0), (WINDOW, 0), (0, 0)))
        vw_pad = jnp.pad(vw_p, ((0, 0), (WINDOW, 0), (0, 0)))

        def nsa_block(qs):
            q = lax.dynamic_slice_in_dim(nq_p, qs, QBLK, axis=1)
            g = lax.dynamic_slice_in_dim(ng_p, qs, QBLK, axis=1)
            kwb = lax.dynamic_slice_in_dim(kw_pad, qs, WINDOW + QBLK, axis=1)
            vwb = lax.dynamic_slice_in_dim(vw_pad, qs, WINDOW + QBLK, axis=1)
            wpos = qs - WINDOW + jnp.arange(WINDOW + QBLK, dtype=jnp.int32)
            return nsa_attend(q, qs + jnp.arange(QBLK, dtype=jnp.int32), kc_p, vc_p, cend_p, ksb_p, vsb_p, kwb, vwb, wpos, g)
        o_nsa_p = sweep_query_blocks(nsa_block, t_p)
        xp = merge_and_mlp(xp, o_mla_p, o_nsa_p, ga_p, gb_p, *out_l)
        p_rows.append((c_p, kr_p, kcr_p, vcr_p, ks_p, vs_p, kw_p[:, t_p - wkeep_p:], vw_p[:, t_p - wkeep_p:]))

        hs = rms_norm(xs, g_attn[l])
        (qn_s, qr_s, c_s, kr_s, nq_s, kcr_s, vcr_s, ks_s, vs_s, kw_s, vw_s, ng_s, ga_s, gb_s) = project_inputs(hs, pos_s, *lw)
        c_all = jnp.concatenate([gather_pages(cache_mla_latent, l, page_table), c_s], axis=1)
        kr_all = jnp.concatenate([gather_pages(cache_mla_krope, l, page_table), kr_s], axis=1)
        r_all = mla_key_scale(c_all, w_kb[l])
        o_mla_s = mla_attend(qn_s, qr_s, pos_s, c_all, kr_all, r_all, pos_all, w_kb[l], w_vb[l], g_kn[l])

        kcr_all = jnp.concatenate([gather_pages(cache_nsa_kcmp, l, page_table), kcr_s], axis=1)
        vcr_all = jnp.concatenate([gather_pages(cache_nsa_vcmp, l, page_table), vcr_s], axis=1)
        kc_s, vc_s, cend_s = compressed_branch(kcr_all, vcr_all, *cmp_l)
        ksb_s = to_blocks(jnp.concatenate([gather_pages(cache_nsa_kslc, l, page_table), ks_s], axis=1))
        vsb_s = to_blocks(jnp.concatenate([gather_pages(cache_nsa_vslc, l, page_table), vs_s], axis=1))
        kw_all = jnp.concatenate([state_nsa_kwin[l], kw_s], axis=1)
        vw_all = jnp.concatenate([state_nsa_vwin[l], vw_s], axis=1)
        o_nsa_s = nsa_attend(nq_s, pos_s, kc_s, vc_s, cend_s, ksb_s, vsb_s, kw_all, vw_all, wpos_s, ng_s)
        xs = merge_and_mlp(xs, o_mla_s, o_nsa_s, ga_s, gb_s, *out_l)
        s_rows.append((c_s, kr_s, kcr_s, vcr_s, ks_s, vs_s, kw_all[:, t_s:], vw_all[:, t_s:]))

    (lat_p, krope_p, kcmp_p, vcmp_p, kslc_p, vslc_p, kwin_p, vwin_p) = [jnp.stack(a, axis=0) for a in zip(*p_rows)]
    (lat_s, krope_s, kcmp_s, vcmp_s, kslc_s, vslc_s, kwin_s, vwin_s) = [jnp.stack(a, axis=0) for a in zip(*s_rows)]
    return (xp, xs, lat_p, lat_s, krope_p, krope_s, kcmp_p, kcmp_s, vcmp_p, vcmp_s,
            kslc_p, kslc_s, vslc_p, vslc_s, kwin_p, kwin_s, vwin_p, vwin_s)
```

```python
import functools

import jax
import jax.numpy as jnp
import numpy as np
from jax import lax
from jax.experimental import pallas as pl
from jax.experimental.pallas import tpu as pltpu

CMP_STRIDE = 16
SLC_BLOCK = 64
N_SELECT = 16
WINDOW = 512
QBLK = 128
ROPE_THETA = 500000.0
EPS = 1e-6
NEG = -1e30
FORCE_SCORE = 1e9

F32 = jnp.float32
BF16 = jnp.bfloat16

VMEM_LIMIT_BYTES = 56 * 1024 * 1024


def _mm_body(*refs, nk, epilogue, n_extra):
    a_ref, b_ref = refs[0], refs[1]
    extras = refs[2:2 + n_extra]
    o_ref = refs[2 + n_extra]
    acc_ref = refs[3 + n_extra] if nk > 1 else None

    part = jnp.dot(a_ref[...].astype(BF16), b_ref[...].astype(BF16), preferred_element_type=F32)

    def finish(acc):
        if epilogue == "none":
            r = acc
        elif epilogue == "relu2":
            r = jnp.square(jnp.maximum(acc, 0.0))
        elif epilogue == "add":
            r = acc + extras[0][...]
        elif epilogue == "gate":
            r = jax.nn.sigmoid(extras[0][...]) * acc
        elif epilogue == "gate_add":
            r = jax.nn.sigmoid(extras[0][...]) * acc + extras[1][...]
        else:
            raise ValueError(epilogue)
        o_ref[...] = r.astype(o_ref.dtype)

    if nk == 1:
        finish(part)
        return

    k = pl.program_id(2)

    @pl.when(k == 0)
    def _():
        acc_ref[...] = part

    @pl.when(k > 0)
    def _():
        acc_ref[...] += part

    @pl.when(k == nk - 1)
    def _():
        finish(acc_ref[...])


def _pick_tile(dim, target, quantum):
    best = None
    t = quantum
    while t <= min(dim, target):
        if dim % t == 0:
            best = t
        t += quantum
    return best if best is not None else dim


def matmul(a, b, *, epilogue="none", extras=(), out_dtype=F32, tm=1664, tn=1024, tk=1024):
    m, kdim = a.shape
    kdim2, n = b.shape
    assert kdim == kdim2
    tm = _pick_tile(m, tm, 128) if m % 128 == 0 else m
    tk = _pick_tile(kdim, tk, 128)
    tn = min(tn, n)
    nk = kdim // tk
    grid = (m // tm, pl.cdiv(n, tn), nk)
    in_specs = [
        pl.BlockSpec((tm, tk), lambda i, j, k: (i, k)),
        pl.BlockSpec((tk, tn), lambda i, j, k: (k, j)),
    ]
    for _ in extras:
        in_specs.append(pl.BlockSpec((tm, tn), lambda i, j, k: (i, j)))
    scratch = [pltpu.VMEM((tm, tn), F32)] if nk > 1 else []
    return pl.pallas_call(
        functools.partial(_mm_body, nk=nk, epilogue=epilogue, n_extra=len(extras)),
        grid=grid,
        in_specs=in_specs,
        out_specs=pl.BlockSpec((tm, tn), lambda i, j, k: (i, j)),
        out_shape=jax.ShapeDtypeStruct((m, n), out_dtype),
        scratch_shapes=scratch,
        compiler_params=pltpu.CompilerParams(
            dimension_semantics=("parallel", "parallel", "arbitrary"),
            vmem_limit_bytes=VMEM_LIMIT_BYTES),
        name="mm_" + epilogue,
    )(a, b, *extras)


def rms_norm(x, g):
    xf = x.astype(F32)
    y = xf * lax.rsqrt(jnp.mean(xf * xf, axis=-1, keepdims=True) + EPS)
    return (y * g.astype(F32)).astype(x.dtype)


def apply_rope(x, pos, rot_dim):
    half = rot_dim // 2
    inv = ROPE_THETA ** (-jnp.arange(half, dtype=F32) / half)
    ang = pos.astype(F32)[:, None] * inv[None, :]
    shape = (ang.shape[0],) + (1,) * (x.ndim - 3) + (half,)
    cos = jnp.cos(ang).reshape(shape).astype(x.dtype)
    sin = jnp.sin(ang).reshape(shape).astype(x.dtype)
    x1, x2, rest = x[..., :half], x[..., half:rot_dim], x[..., rot_dim:]
    return jnp.concatenate([x1 * cos - x2 * sin, x2 * cos + x1 * sin, rest], axis=-1)


def masked_softmax(s, mask):
    s = jnp.where(mask, s.astype(F32), NEG)
    m = jnp.max(s, axis=-1, keepdims=True)
    e = jnp.where(mask, jnp.exp(s - m), 0.0)
    return e / jnp.maximum(jnp.sum(e, axis=-1, keepdims=True), 1e-30)


def gather_pages(pool, layer, page_table):
    rows = pool[layer, page_table]
    return rows.reshape(page_table.shape[0], -1, pool.shape[-1])


def sweep_query_blocks(fn, t):
    starts = jnp.arange(t // QBLK, dtype=jnp.int32) * QBLK
    out = lax.map(fn, starts)
    nb, b, qb, f = out.shape
    return jnp.transpose(out, (1, 0, 2, 3)).reshape(b, nb * qb, f)


def mla_key_scale(c, w_kb):
    def per_head(w_h):
        k = jnp.einsum('btr,rd->btd', c, w_h).astype(F32)
        return lax.rsqrt(jnp.mean(k * k, axis=-1) + EPS)
    return jnp.transpose(lax.map(per_head, w_kb), (1, 0, 2))


def mla_attend(q_nope, q_rope, qpos, c, k_rope, r_k, kpos, w_kb, w_vb, g_kn, scale):
    b, tq, h, _ = q_nope.shape
    q_abs = jnp.einsum('bqhd,hrd->bqhr', q_nope * g_kn, w_kb)
    s = jnp.einsum('bqhr,bkr->bhqk', q_abs, c) * r_k[:, :, None, :].astype(c.dtype)
    s = (s + jnp.einsum('bqhd,bkd->bhqk', q_rope, k_rope)) * scale
    p = masked_softmax(s, kpos[None, :] <= qpos[:, None])
    o_lat = jnp.einsum('bhqk,bkr->bqhr', p.astype(c.dtype), c)
    return jnp.einsum('bqhr,hrv->bqhv', o_lat, w_vb).reshape(b, tq, -1)


def compress_rows(rows, pos_emb, w1, b1, w2):
    b, t, dk = rows.shape
    cmp_len = pos_emb.shape[0]
    nc = (t - cmp_len) // CMP_STRIDE + 1
    idx = jnp.arange(nc)[:, None] * CMP_STRIDE + jnp.arange(cmp_len)[None, :]
    blocks = (rows[:, idx] + pos_emb).reshape(b, nc, cmp_len * dk)
    return jax.nn.gelu(blocks @ w1 + b1) @ w2


def compressed_branch(k_raw, v_raw, g_nkc_l, cmp_pos_l, cmp_w1_l, cmp_b1_l, cmp_w2_l, nsa_rot):
    cmp_len = cmp_pos_l.shape[1]
    kc = compress_rows(k_raw, cmp_pos_l[0], cmp_w1_l[0], cmp_b1_l[0], cmp_w2_l[0])
    vc = compress_rows(v_raw, cmp_pos_l[1], cmp_w1_l[1], cmp_b1_l[1], cmp_w2_l[1])
    cend = jnp.arange(kc.shape[1]) * CMP_STRIDE + cmp_len - 1
    kc = apply_rope(rms_norm(kc, g_nkc_l), cend, nsa_rot)
    return kc, vc, cend


def to_blocks(rows):
    b, t, d = rows.shape
    ns = -(-t // SLC_BLOCK)
    rows = jnp.pad(rows, ((0, 0), (0, ns * SLC_BLOCK - t), (0, 0)))
    return rows.reshape(b, ns, SLC_BLOCK, d)


def nsa_attend(q, qpos, kc, vc, cend, cmp_len, ks_blk, vs_blk, kw, vw, wpos, gate_logits):
    b, tq, h, dk = q.shape
    scale = dk ** -0.5
    s_c = jnp.einsum('bqhd,bnd->bhqn', q, kc) * scale
    p_c = masked_softmax(s_c, cend[None, :] <= qpos[:, None])
    o_c = jnp.einsum('bhqn,bnd->bqhd', p_c.astype(vc.dtype), vc)
    ns = ks_blk.shape[1]
    blk = jnp.arange(ns)
    bstart = blk * SLC_BLOCK
    cstart = cend - (cmp_len - 1)
    overlap = ((cstart[:, None] < bstart[None, :] + SLC_BLOCK) & (cend[:, None] >= bstart[None, :])).astype(F32)
    imp = jnp.einsum('bhqn,ns->bqs', p_c, overlap)
    cur = qpos // SLC_BLOCK
    forced = (blk[None, :] == 0) | (blk[None, :] == cur[:, None]) | (blk[None, :] == cur[:, None] - 1)
    valid = bstart[None, :] <= qpos[:, None]
    imp = jnp.where(forced[None], FORCE_SCORE, jnp.where(valid[None], imp, -FORCE_SCORE))
    _, sel = lax.top_k(imp, min(N_SELECT, ns))
    bidx = jnp.arange(b)[:, None, None]
    ksel = ks_blk[bidx, sel].reshape(b, tq, -1, dk)
    vsel = vs_blk[bidx, sel].reshape(b, tq, -1, dk)
    spos = (sel[..., None] * SLC_BLOCK + jnp.arange(SLC_BLOCK)).reshape(b, tq, -1)
    s_s = jnp.einsum('bqhd,bqmd->bhqm', q, ksel) * scale
    p_s = masked_softmax(s_s, (spos <= qpos[None, :, None])[:, None])
    o_s = jnp.einsum('bhqm,bqmd->bqhd', p_s.astype(vsel.dtype), vsel)
    s_w = jnp.einsum('bqhd,bkd->bhqk', q, kw) * scale
    dist = qpos[:, None] - wpos[None, :]
    p_w = masked_softmax(s_w, (dist >= 0) & (dist < WINDOW) & (wpos[None, :] >= 0))
    o_w = jnp.einsum('bhqk,bkd->bqhd', p_w.astype(vw.dtype), vw)
    g = jax.nn.sigmoid(gate_logits.astype(F32)).astype(q.dtype)
    o = g[..., 0:1] * o_c + g[..., 1:2] * o_s + g[..., 2:3] * o_w
    return o.reshape(b, tq, h * dk)


def kernel(x_prompt, x_sample, cache_mla_latent, cache_mla_krope, cache_nsa_kcmp, cache_nsa_vcmp,
           cache_nsa_kslc, cache_nsa_vslc, state_nsa_kwin, state_nsa_vwin, page_table,
           g_attn, w_in, g_qa, w_qb, g_kva, g_qn, g_qr, g_kn, g_kr, w_kb, w_vb, w_mla_o,
           g_nq, g_nkc, g_nks, g_nkw, cmp_pos, cmp_w1, cmp_b1, cmp_w2, w_nsa_o,
           w_out, g_mlp, w_up, w_down):
    nb, t_p, d_model = x_prompt.shape
    db, t_s, _ = x_sample.shape
    depth = w_in.shape[0]
    q_rank = g_qa.shape[1]
    kv_rank = g_kva.shape[1]
    mla_nope = g_qn.shape[1]
    mla_rope = g_qr.shape[1]
    mla_heads = w_kb.shape[1]
    nsa_dk = g_nq.shape[1]
    nsa_heads = w_nsa_o.shape[1] // nsa_dk
    nsa_rot = nsa_dk // 4
    cmp_len = cmp_pos.shape[2]
    page_size = cache_mla_latent.shape[2]
    past_len = page_table.shape[1] * page_size
    wbuf = state_nsa_kwin.shape[2]
    wkeep_p = min(WINDOW, t_p)
    mla_scale = (mla_nope + mla_rope) ** -0.5

    n_p = nb * t_p
    n_s = db * t_s
    pos_p = jnp.arange(t_p, dtype=jnp.int32)
    pos_s = past_len + jnp.arange(t_s, dtype=jnp.int32)
    pos_all = jnp.arange(past_len + t_s, dtype=jnp.int32)
    wpos_s = past_len - wbuf + jnp.arange(wbuf + t_s, dtype=jnp.int32)

    widths = [q_rank, kv_rank, mla_rope, nsa_heads * nsa_dk] + [nsa_dk] * 6 + [3 * nsa_heads, d_model, d_model]
    off = np.cumsum([0] + widths)

    x = jnp.concatenate([x_prompt.reshape(n_p, d_model), x_sample.reshape(n_s, d_model)], axis=0)
    p_rows, s_rows = [], []

    def split(a):
        return a[:n_p].reshape(nb, t_p, *a.shape[1:]), a[n_p:].reshape(db, t_s, *a.shape[1:])

    for l in range(depth):
        h = rms_norm(x, g_attn[l]).astype(BF16)
        z = matmul(h, w_in[l])
        parts = [z[:, int(off[i]):int(off[i + 1])] for i in range(len(widths))]
        q_lat, kv_lat, kr, nq, kc_raw, vc_raw, ks, vs, kw, vw, ngate, gate_a, gate_b = parts
        qa = rms_norm(q_lat, g_qa[l]).astype(BF16)
        q = matmul(qa, w_qb[l], tk=q_rank).reshape(-1, mla_heads, mla_nope + mla_rope)
        q_nope = rms_norm(q[..., :mla_nope], g_qn[l])
        q_rope_n = rms_norm(q[..., mla_nope:], g_qr[l])
        c = rms_norm(kv_lat, g_kva[l])
        kr_n = rms_norm(kr, g_kr[l])
        nq_n = rms_norm(nq.reshape(-1, nsa_heads, nsa_dk), g_nq[l])
        ks_n = rms_norm(ks, g_nks[l])
        kw_n = rms_norm(kw, g_nkw[l])
        ngate = ngate.reshape(-1, nsa_heads, 3)

        (qn_p, qn_s), (qrn_p, qrn_s), (c_p, c_s), (krn_p, krn_s) = split(q_nope), split(q_rope_n), split(c), split(kr_n)
        (nqn_p, nqn_s), (kcr_p, kcr_s), (vcr_p, vcr_s) = split(nq_n), split(kc_raw), split(vc_raw)
        (ksn_p, ksn_s), (vs_p, vs_s), (kwn_p, kwn_s), (vw_p, vw_s) = split(ks_n), split(vs), split(kw_n), split(vw)
        ng_p, ng_s = split(ngate)

        cmp_l = (g_nkc[l], cmp_pos[l], cmp_w1[l], cmp_b1[l], cmp_w2[l], nsa_rot)

        qr_p = apply_rope(qrn_p, pos_p, mla_rope)
        kr_p = apply_rope(krn_p, pos_p, mla_rope)
        nq_p = apply_rope(nqn_p, pos_p, nsa_rot)
        ks_p = apply_rope(ksn_p, pos_p, nsa_rot)
        kw_p = apply_rope(kwn_p, pos_p, nsa_rot)
        r_p = mla_key_scale(c_p, w_kb[l])

        def mla_block(qs):
            q1 = lax.dynamic_slice_in_dim(qn_p, qs, QBLK, axis=1)
            q2 = lax.dynamic_slice_in_dim(qr_p, qs, QBLK, axis=1)
            return mla_attend(q1, q2, qs + jnp.arange(QBLK, dtype=jnp.int32), c_p, kr_p, r_p, pos_p,
                              w_kb[l], w_vb[l], g_kn[l], mla_scale)
        o_mla_p = sweep_query_blocks(mla_block, t_p)

        kc_p, vc_p, cend_p = compressed_branch(kcr_p, vcr_p, *cmp_l)
        ksb_p, vsb_p = to_blocks(ks_p), to_blocks(vs_p)
        kw_pad = jnp.pad(kw_p, ((0, 0), (WINDOW, 0), (0, 0)))
        vw_pad = jnp.pad(vw_p, ((0, 0), (WINDOW, 0), (0, 0)))

        def nsa_block(qs):
            qq = lax.dynamic_slice_in_dim(nq_p, qs, QBLK, axis=1)
            g = lax.dynamic_slice_in_dim(ng_p, qs, QBLK, axis=1)
            kwb = lax.dynamic_slice_in_dim(kw_pad, qs, WINDOW + QBLK, axis=1)
            vwb = lax.dynamic_slice_in_dim(vw_pad, qs, WINDOW + QBLK, axis=1)
            wpos = qs - WINDOW + jnp.arange(WINDOW + QBLK, dtype=jnp.int32)
            return nsa_attend(qq, qs + jnp.arange(QBLK, dtype=jnp.int32), kc_p, vc_p, cend_p, cmp_len,
                              ksb_p, vsb_p, kwb, vwb, wpos, g)
        o_nsa_p = sweep_query_blocks(nsa_block, t_p)
        p_rows.append((c_p, kr_p, kcr_p, vcr_p, ks_p, vs_p, kw_p[:, t_p - wkeep_p:], vw_p[:, t_p - wkeep_p:]))

        qr_s = apply_rope(qrn_s, pos_s, mla_rope)
        kr_s = apply_rope(krn_s, pos_s, mla_rope)
        nq_s = apply_rope(nqn_s, pos_s, nsa_rot)
        ks_s = apply_rope(ksn_s, pos_s, nsa_rot)
        kw_s = apply_rope(kwn_s, pos_s, nsa_rot)
        c_all = jnp.concatenate([gather_pages(cache_mla_latent, l, page_table), c_s], axis=1)
        kr_all = jnp.concatenate([gather_pages(cache_mla_krope, l, page_table), kr_s], axis=1)
        r_all = mla_key_scale(c_all, w_kb[l])
        o_mla_s = mla_attend(qn_s, qr_s, pos_s, c_all, kr_all, r_all, pos_all, w_kb[l], w_vb[l], g_kn[l], mla_scale)

        kcr_all = jnp.concatenate([gather_pages(cache_nsa_kcmp, l, page_table), kcr_s], axis=1)
        vcr_all = jnp.concatenate([gather_pages(cache_nsa_vcmp, l, page_table), vcr_s], axis=1)
        kc_s, vc_s, cend_s = compressed_branch(kcr_all, vcr_all, *cmp_l)
        ksb_s = to_blocks(jnp.concatenate([gather_pages(cache_nsa_kslc, l, page_table), ks_s], axis=1))
        vsb_s = to_blocks(jnp.concatenate([gather_pages(cache_nsa_vslc, l, page_table), vs_s], axis=1))
        kw_all = jnp.concatenate([state_nsa_kwin[l], kw_s], axis=1)
        vw_all = jnp.concatenate([state_nsa_vwin[l], vw_s], axis=1)
        o_nsa_s = nsa_attend(nq_s, pos_s, kc_s, vc_s, cend_s, cmp_len, ksb_s, vsb_s, kw_all, vw_all, wpos_s, ng_s)
        s_rows.append((c_s, kr_s, kcr_s, vcr_s, ks_s, vs_s, kw_all[:, t_s:], vw_all[:, t_s:]))

        o_mla = jnp.concatenate([o_mla_p.reshape(n_p, -1), o_mla_s.reshape(n_s, -1)], axis=0).astype(BF16)
        o_nsa = jnp.concatenate([o_nsa_p.reshape(n_p, -1), o_nsa_s.reshape(n_s, -1)], axis=0).astype(BF16)
        y1 = matmul(o_mla, w_mla_o[l], epilogue="gate", extras=(gate_a,), tn=512)
        y = matmul(o_nsa, w_nsa_o[l], epilogue="gate_add", extras=(gate_b, y1), out_dtype=BF16, tn=512)
        x = matmul(y, w_out[l], epilogue="add", extras=(x,), tn=512)
        h2 = rms_norm(x, g_mlp[l]).astype(BF16)
        u = matmul(h2, w_up[l], epilogue="relu2", out_dtype=BF16)
        x = matmul(u, w_down[l], epilogue="add", extras=(x,), tn=512)

    xp = x[:n_p].reshape(nb, t_p, d_model)
    xs = x[n_p:].reshape(db, t_s, d_model)
    (lat_p, krope_p, kcmp_p, vcmp_p, kslc_p, vslc_p, kwin_p, vwin_p) = [jnp.stack(a, axis=0) for a in zip(*p_rows)]
    (lat_s, krope_s, kcmp_s, vcmp_s, kslc_s, vslc_s, kwin_s, vwin_s) = [jnp.stack(a, axis=0) for a in zip(*s_rows)]
    return (xp, xs, lat_p, lat_s, krope_p, krope_s, kcmp_p, kcmp_s, vcmp_p, vcmp_s,
            kslc_p, kslc_s, vslc_p, vslc_s, kwin_p, kwin_s, vwin_p, vwin_s)
```

```python
import functools

import jax
import jax.numpy as jnp
import numpy as np
from jax import lax
from jax.experimental import pallas as pl
from jax.experimental.pallas import tpu as pltpu

CMP_STRIDE = 16
SLC_BLOCK = 64
N_SELECT = 16
WINDOW = 512
ROPE_THETA = 500000.0
EPS = 1e-6
NEG = -1e30
FORCE_SCORE = 1e9

F32 = jnp.float32
BF16 = jnp.bfloat16
LANES = 128
ATT_TQ = 128

VMEM_LIMIT_BYTES = 56 * 1024 * 1024

_NT = (((1,), (1,)), ((), ()))


def _cparams(*sem):
    return pltpu.CompilerParams(dimension_semantics=sem, vmem_limit_bytes=VMEM_LIMIT_BYTES)


def _pick_tile(dim, target, quantum):
    best = None
    t = quantum
    while t <= min(dim, target):
        if dim % t == 0:
            best = t
        t += quantum
    return best if best is not None else dim


def _mm_body(*refs, nk, epilogue, n_extra):
    a_ref, b_ref = refs[0], refs[1]
    extras = refs[2:2 + n_extra]
    o_ref = refs[2 + n_extra]
    acc_ref = refs[3 + n_extra] if nk > 1 else None

    part = jnp.dot(a_ref[...].astype(BF16), b_ref[...].astype(BF16), preferred_element_type=F32)

    def finish(acc):
        if epilogue == "none":
            r = acc
        elif epilogue == "relu2":
            r = jnp.square(jnp.maximum(acc, 0.0))
        elif epilogue == "add":
            r = acc + extras[0][...]
        elif epilogue == "gate":
            r = jax.nn.sigmoid(extras[0][...]) * acc
        elif epilogue == "gate_add":
            r = jax.nn.sigmoid(extras[0][...]) * acc + extras[1][...]
        else:
            raise ValueError(epilogue)
        o_ref[...] = r.astype(o_ref.dtype)

    if nk == 1:
        finish(part)
        return

    k = pl.program_id(2)

    @pl.when(k == 0)
    def _():
        acc_ref[...] = part

    @pl.when(k > 0)
    def _():
        acc_ref[...] += part

    @pl.when(k == nk - 1)
    def _():
        finish(acc_ref[...])


def matmul(a, b, *, epilogue="none", extras=(), extra_col_blocks=None, out_dtype=F32, tm=1664, tn=1024, tk=1024):
    m, kdim = a.shape
    kdim2, n = b.shape
    assert kdim == kdim2
    tm = _pick_tile(m, tm, LANES)
    tk = _pick_tile(kdim, tk, LANES)
    tn = min(tn, n)
    nk = kdim // tk
    grid = (m // tm, pl.cdiv(n, tn), nk)
    in_specs = [
        pl.BlockSpec((tm, tk), lambda i, j, k: (i, k)),
        pl.BlockSpec((tk, tn), lambda i, j, k: (k, j)),
    ]
    offs = extra_col_blocks or (0,) * len(extras)
    for o in offs:
        in_specs.append(pl.BlockSpec((tm, tn), lambda i, j, k, o=o: (i, j + o)))
    scratch = [pltpu.VMEM((tm, tn), F32)] if nk > 1 else []
    return pl.pallas_call(
        functools.partial(_mm_body, nk=nk, epilogue=epilogue, n_extra=len(extras)),
        grid=grid,
        in_specs=in_specs,
        out_specs=pl.BlockSpec((tm, tn), lambda i, j, k: (i, j)),
        out_shape=jax.ShapeDtypeStruct((m, n), out_dtype),
        scratch_shapes=scratch,
        compiler_params=_cparams("parallel", "parallel", "arbitrary"),
        name="mm_" + epilogue,
    )(a, b, *extras)


def _group_mean_sq(x, group):
    w = x.shape[1]
    xx = x * x
    hi = xx.astype(BF16)
    lo = (xx - hi.astype(F32)).astype(BF16)
    r = lax.broadcasted_iota(jnp.int32, (LANES, LANES), 0) // group
    c = lax.broadcasted_iota(jnp.int32, (LANES, LANES), 1) // group
    bd = jnp.where(r == c, 1.0, 0.0).astype(BF16)
    outs = []
    for j in range(w // LANES):
        sl = slice(j * LANES, (j + 1) * LANES)
        outs.append(jnp.dot(hi[:, sl], bd, preferred_element_type=F32)
                    + jnp.dot(lo[:, sl], bd, preferred_element_type=F32))
    ms = outs[0] if len(outs) == 1 else jnp.concatenate(outs, axis=1)
    return ms * (1.0 / group)


def _rope128(y, cos, sin, half, group):
    lane = lax.broadcasted_iota(jnp.int32, y.shape, 1) % group
    partner = jnp.where(lane < half, pltpu.roll(y, LANES - half, 1), pltpu.roll(y, half, 1))
    return y * cos + partner * sin


def _row_rms(x):
    return lax.rsqrt(jnp.mean(x * x, axis=-1, keepdims=True) + EPS)


def _p1_body(z_ref, gqa_ref, gkva_ref, gnq_ref, ga_ref, gb_ref, cn_ref, sn_ref, cm_ref, sm_ref,
             qa_ref, c_ref, nq_ref, ta_ref, tb_ref, *, q_rank, kv_rank, nqw, dk, nsa_half, mla_half, nsa_scale):
    o1 = q_rank + kv_rank
    x = z_ref[:, 0:q_rank]
    qa_ref[...] = (x * _row_rms(x) * gqa_ref[...]).astype(qa_ref.dtype)
    x = z_ref[:, q_rank:o1]
    c_ref[...] = x * _row_rms(x) * gkva_ref[...]
    cn, sn = cn_ref[...], sn_ref[...]
    heads_per_tile = LANES // dk
    for j in range(nqw // LANES):
        x = z_ref[:, o1 + j * LANES:o1 + (j + 1) * LANES]
        y = x * lax.rsqrt(_group_mean_sq(x, dk) + EPS) * gnq_ref[:, j * LANES:(j + 1) * LANES]
        y = _rope128(y, cn, sn, nsa_half, dk) * nsa_scale
        for hh in range(heads_per_tile):
            nq_ref[j * heads_per_tile + hh] = y[:, hh * dk:(hh + 1) * dk].astype(nq_ref.dtype)
    o2 = o1 + nqw
    x = z_ref[:, o2:o2 + LANES]
    y = x * lax.rsqrt(_group_mean_sq(x, dk) + EPS) * ga_ref[...]
    ta_ref[...] = _rope128(y, cn, sn, nsa_half, dk)
    x = z_ref[:, o2 + LANES:o2 + 2 * LANES]
    y = x * lax.rsqrt(_group_mean_sq(x, 2 * mla_half) + EPS) * gb_ref[...]
    tb_ref[...] = _rope128(y, cm_ref[...], sm_ref[...], mla_half, 2 * mla_half)


def post_in_proj(z, gqa, gkva, gnq_t, g_a, g_b, cn, sn, cm, sm, *, q_rank, kv_rank, nsa_heads, dk, mla_rope):
    m = z.shape[0]
    nqw = nsa_heads * dk
    win = q_rank + kv_rank + nqw + 2 * LANES
    tm = _pick_tile(m, 640, LANES)
    row = lambda i: (i, 0)
    cst = lambda i: (0, 0)
    return pl.pallas_call(
        functools.partial(_p1_body, q_rank=q_rank, kv_rank=kv_rank, nqw=nqw, dk=dk, nsa_half=dk // 8,
                          mla_half=mla_rope // 2, nsa_scale=dk ** -0.5),
        grid=(m // tm,),
        in_specs=[pl.BlockSpec((tm, win), row),
                  pl.BlockSpec((1, q_rank), cst), pl.BlockSpec((1, kv_rank), cst), pl.BlockSpec((1, nqw), cst),
                  pl.BlockSpec((1, LANES), cst), pl.BlockSpec((1, LANES), cst),
                  pl.BlockSpec((tm, LANES), row), pl.BlockSpec((tm, LANES), row),
                  pl.BlockSpec((tm, LANES), row), pl.BlockSpec((tm, LANES), row)],
        out_specs=[pl.BlockSpec((tm, q_rank), row), pl.BlockSpec((tm, kv_rank), row),
                   pl.BlockSpec((nsa_heads, tm, dk), lambda i: (0, i, 0)),
                   pl.BlockSpec((tm, LANES), row), pl.BlockSpec((tm, LANES), row)],
        out_shape=[jax.ShapeDtypeStruct((m, q_rank), BF16), jax.ShapeDtypeStruct((m, kv_rank), F32),
                   jax.ShapeDtypeStruct((nsa_heads, m, dk), BF16),
                   jax.ShapeDtypeStruct((m, LANES), F32), jax.ShapeDtypeStruct((m, LANES), F32)],
        compiler_params=_cparams("parallel"),
        name="post_in_proj",
    )(z, gqa, gkva, gnq_t, g_a, g_b, cn, sn, cm, sm)


def _p2_body(q_ref, gn_ref, gr_ref, wkb_ref, cm_ref, sm_ref, qabs_ref, qr_ref, *, heads, nope, rope, scale):
    for h in range(heads):
        x = q_ref[:, h * nope:(h + 1) * nope]
        y = x * lax.rsqrt(_group_mean_sq(x, nope) + EPS) * gn_ref[...] * scale
        qa = lax.dot_general(y.astype(BF16), wkb_ref[h].astype(BF16), _NT, preferred_element_type=F32)
        qabs_ref[h] = qa.astype(qabs_ref.dtype)
    o = heads * nope
    per_tile = LANES // rope
    cm, sm = cm_ref[...], sm_ref[...]
    for j in range(heads * rope // LANES):
        x = q_ref[:, o + j * LANES:o + (j + 1) * LANES]
        y = x * lax.rsqrt(_group_mean_sq(x, rope) + EPS) * gr_ref[...]
        y = _rope128(y, cm, sm, rope // 2, rope) * scale
        for hh in range(per_tile):
            qr_ref[j * per_tile + hh] = y[:, hh * rope:(hh + 1) * rope].astype(qr_ref.dtype)


def post_q_proj(q, gn, gr, w_kb_l, cm, sm, *, heads, nope, rope, kv_rank, scale):
    m = q.shape[0]
    tm = _pick_tile(m, 640, LANES)
    row = lambda i: (i, 0)
    cst = lambda i: (0, 0)
    return pl.pallas_call(
        functools.partial(_p2_body, heads=heads, nope=nope, rope=rope, scale=scale),
        grid=(m // tm,),
        in_specs=[pl.BlockSpec((tm, heads * (nope + rope)), row),
                  pl.BlockSpec((1, nope), cst), pl.BlockSpec((1, LANES), cst),
                  pl.BlockSpec((heads, kv_rank, nope), lambda i: (0, 0, 0)),
                  pl.BlockSpec((tm, LANES), row), pl.BlockSpec((tm, LANES), row)],
        out_specs=[pl.BlockSpec((heads, tm, kv_rank), lambda i: (0, i, 0)),
                   pl.BlockSpec((heads, tm, rope), lambda i: (0, i, 0))],
        out_shape=[jax.ShapeDtypeStruct((heads, m, kv_rank), BF16), jax.ShapeDtypeStruct((heads, m, rope), BF16)],
        compiler_params=_cparams("parallel"),
        name="post_q_proj",
    )(q, gn, gr, w_kb_l, cm, sm)


def _key_scale(c_bf, wt_bf, heads, nope):
    kt = lax.dot_general(wt_bf, c_bf, _NT, preferred_element_type=F32)
    kt = kt * kt
    ss = jnp.sum(kt.reshape(heads, nope, kt.shape[1]), axis=1)
    return lax.rsqrt(ss * (1.0 / nope) + EPS)


def _ks_body(c_ref, wt_ref, r_ref, *, heads, nope):
    r_ref[...] = _key_scale(c_ref[...].astype(BF16), wt_ref[...], heads, nope)


def prompt_key_scale(c, wt_bf, *, nb, t, heads, nope, tk):
    kv_rank = c.shape[1]
    nch = t // tk
    return pl.pallas_call(
        functools.partial(_ks_body, heads=heads, nope=nope),
        grid=(nb, nch),
        in_specs=[pl.BlockSpec((tk, kv_rank), lambda b, j: (b * nch + j, 0)),
                  pl.BlockSpec((heads * nope, kv_rank), lambda b, j: (0, 0))],
        out_specs=pl.BlockSpec((None, None, heads, tk), lambda b, j: (b, j, 0, 0)),
        out_shape=jax.ShapeDtypeStruct((nb, nch, heads, tk), F32),
        compiler_params=_cparams("parallel", "parallel"),
        name="prompt_key_scale",
    )(c, wt_bf)


def _att_body(*refs, mode, heads, tq, tk, d1, d2, dv, k1_off, k2_off, v_off, has_r):
    it = iter(refs)
    q1_ref = next(it)
    q2_ref = next(it) if d2 else None
    k1_ref = next(it)
    k2_ref = next(it) if d2 else None
    v_ref = next(it)
    r_ref = next(it) if has_r else None
    sel_ref = next(it) if mode == "select" else None
    o_ref, m_scr, l_scr, acc_scr = next(it), next(it), next(it), next(it)

    i = pl.program_id(1)
    q0 = i * tq
    rows = heads * tq
    q1 = q1_ref[...].reshape(rows, d1)
    q2 = q2_ref[...].reshape(rows, d2) if d2 else None

    m_scr[...] = jnp.full(m_scr.shape, NEG, F32)
    l_scr[...] = jnp.zeros(l_scr.shape, F32)
    acc_scr[...] = jnp.zeros(acc_scr.shape, F32)

    hi = (q0 + tq + tk - 1) // tk
    lo = jnp.maximum(q0 - WINDOW, 0) // tk if mode == "window" else 0

    def chunk(kci, carry):
        ks = pl.multiple_of(kci * tk, tk)
        k1c = k1_ref[pl.ds(ks, tk), k1_off:k1_off + d1].astype(BF16)
        s = lax.dot_general(q1, k1c, _NT, preferred_element_type=F32).reshape(heads, tq, tk)
        if has_r:
            s = s * r_ref[kci][:, None, :]
        if d2:
            k2c = k2_ref[pl.ds(ks, tk), k2_off:k2_off + d2].astype(BF16)
            s = s + lax.dot_general(q2, k2c, _NT, preferred_element_type=F32).reshape(heads, tq, tk)
        qpos = q0 + lax.broadcasted_iota(jnp.int32, (tq, tk), 0)
        kpos = ks + lax.broadcasted_iota(jnp.int32, (tq, tk), 1)
        dist = qpos - kpos
        if mode == "window":
            mask = (dist >= 0) & (dist < WINDOW)
        elif mode == "select":
            blk = (ks + lax.broadcasted_iota(jnp.int32, (LANES, tk), 1)) // SLC_BLOCK
            e = jnp.where(blk == lax.broadcasted_iota(jnp.int32, (LANES, tk), 0), 1.0, 0.0).astype(BF16)
            selm = jnp.dot(sel_ref[...], e, preferred_element_type=F32)
            mask = jnp.where(dist >= 0, selm, 0.0) > 0.5
        else:
            mask = dist >= 0
        mask = mask[None]
        s = jnp.where(mask, s, NEG)
        m_prev = m_scr[...]
        m_new = jnp.maximum(m_prev, jnp.max(s, axis=-1, keepdims=True))
        alpha = jnp.exp(m_prev - m_new)
        p = jnp.where(mask, jnp.exp(s - m_new), 0.0)
        l_scr[...] = alpha * l_scr[...] + jnp.sum(p, axis=-1, keepdims=True)
        vc = v_ref[pl.ds(ks, tk), v_off:v_off + dv].astype(BF16)
        pv = jnp.dot(p.reshape(rows, tk).astype(BF16), vc, preferred_element_type=F32)
        acc_scr[...] = alpha * acc_scr[...] + pv.reshape(heads, tq, dv)
        m_scr[...] = m_new
        return carry

    lax.fori_loop(lo, hi, chunk, 0)
    o_ref[...] = (acc_scr[...] / l_scr[...]).astype(o_ref.dtype)


def prompt_attention(q1, k1, v, *, mode, nb, t, tk, d1, dv, k1_off=0, v_off=0, q2=None, k2=None, d2=0, k2_off=0,
                     r=None, sel=None, v_col_block=0):
    heads = q1.shape[0]
    tq = ATT_TQ
    nq = t // tq
    qmap = lambda b, i: (0, b * nq + i, 0)
    kmap = lambda b, i: (b, 0)
    args, specs = [q1], [pl.BlockSpec((heads, tq, d1), qmap)]
    if d2:
        args.append(q2)
        specs.append(pl.BlockSpec((heads, tq, d2), qmap))
    args.append(k1)
    specs.append(pl.BlockSpec((t, k1.shape[1]), kmap))
    if d2:
        args.append(k2)
        specs.append(pl.BlockSpec((t, k2.shape[1]), kmap))
    args.append(v)
    vw = LANES if v.shape[1] > max(LANES, dv) else v.shape[1]
    specs.append(pl.BlockSpec((t, vw), lambda b, i: (b, v_col_block)))
    if r is not None:
        args.append(r)
        specs.append(pl.BlockSpec((None, t // tk, heads, tk), lambda b, i: (b, 0, 0, 0)))
    if sel is not None:
        args.append(sel)
        specs.append(pl.BlockSpec((tq, LANES), lambda b, i: (b * nq + i, 0)))
    return pl.pallas_call(
        functools.partial(_att_body, mode=mode, heads=heads, tq=tq, tk=tk, d1=d1, d2=d2, dv=dv,
                          k1_off=k1_off, k2_off=k2_off, v_off=v_off, has_r=r is not None),
        grid=(nb, nq),
        in_specs=specs,
        out_specs=pl.BlockSpec((heads, tq, dv), qmap),
        out_shape=jax.ShapeDtypeStruct((heads, q1.shape[1], dv), BF16),
        scratch_shapes=[pltpu.VMEM((heads, tq, 1), F32), pltpu.VMEM((heads, tq, 1), F32),
                        pltpu.VMEM((heads, tq, dv), F32)],
        compiler_params=_cparams("parallel", "parallel"),
        name="prompt_att_" + mode,
    )(*args)


def _compress_core(load_kv, nblk, pos_ref, w1k_ref, w1v_ref, b1_ref, w2k_ref, w2v_ref, g_ref, cos_ref, sin_ref, *, dk, hid):
    half = CMP_STRIDE
    acc = [jnp.zeros((nblk, hid), F32) for _ in range(4)]
    for j in range(half):
        kr, vr = load_kv(j)
        for part, (rows_, w_ref, col) in enumerate(((kr, w1k_ref, 0), (kr, w1k_ref, 0), (vr, w1v_ref, dk), (vr, w1v_ref, dk))):
            jj = j + (half if part % 2 else 0)
            xin = (rows_ + pos_ref[jj:jj + 1, col:col + dk]).astype(BF16)
            acc[part] = acc[part] + jnp.dot(xin, w_ref[jj].astype(BF16), preferred_element_type=F32)
    hk = acc[0] + pltpu.roll(acc[1], nblk - 1, 0) + b1_ref[0:1, :]
    hv = acc[2] + pltpu.roll(acc[3], nblk - 1, 0) + b1_ref[1:2, :]
    kv = (jnp.dot(jax.nn.gelu(hk).astype(BF16), w2k_ref[...].astype(BF16), preferred_element_type=F32)
          + jnp.dot(jax.nn.gelu(hv).astype(BF16), w2v_ref[...].astype(BF16), preferred_element_type=F32))
    lane = lax.broadcasted_iota(jnp.int32, kv.shape, 1)
    y = jnp.where(lane < dk, kv * lax.rsqrt(_group_mean_sq(kv, dk) + EPS) * g_ref[...], kv)
    return _rope128(y, cos_ref[...], sin_ref[...], dk // 8, dk)


def _cmp_prompt_body(rows_ref, pos_ref, w1k_ref, w1v_ref, b1_ref, w2k_ref, w2v_ref, g_ref, cos_ref, sin_ref, o_ref,
                     *, nblk, dk, hid):
    def load_kv(j):
        rows_ = rows_ref[pl.ds(j, nblk, stride=CMP_STRIDE), :]
        return rows_[:, :dk], rows_[:, dk:2 * dk]
    o_ref[...] = _compress_core(load_kv, nblk, pos_ref, w1k_ref, w1v_ref, b1_ref, w2k_ref, w2v_ref, g_ref,
                                cos_ref, sin_ref, dk=dk, hid=hid)


def _cmp_weight_specs(cmp_len, dk, hid, nblk, nidx):
    z2 = (lambda *a: (0, 0))
    z3 = (lambda *a: (0, 0, 0))
    return [pl.BlockSpec((cmp_len, 2 * dk), z2),
            pl.BlockSpec((cmp_len, dk, hid), z3), pl.BlockSpec((cmp_len, dk, hid), z3),
            pl.BlockSpec((2, hid), z2),
            pl.BlockSpec((hid, LANES), z2), pl.BlockSpec((hid, LANES), z2),
            pl.BlockSpec((1, LANES), z2),
            pl.BlockSpec((nblk, LANES), z2), pl.BlockSpec((nblk, LANES), z2)]


def compress_prompt(z, cmpw, *, nb, t, col_block, dk, hid, cmp_len):
    nblk = t // CMP_STRIDE
    return pl.pallas_call(
        functools.partial(_cmp_prompt_body, nblk=nblk, dk=dk, hid=hid),
        grid=(nb,),
        in_specs=[pl.BlockSpec((t, LANES), lambda b: (b, col_block))] + _cmp_weight_specs(cmp_len, dk, hid, nblk, 1),
        out_specs=pl.BlockSpec((None, nblk, LANES), lambda b: (b, 0, 0)),
        out_shape=jax.ShapeDtypeStruct((nb, nblk, LANES), F32),
        compiler_params=_cparams("parallel"),
        name="compress_prompt",
    )(z, *cmpw)


def _select_mask(imp, qpos, ns):
    lane = lax.broadcasted_iota(jnp.int32, imp.shape, 1)
    cur = qpos // SLC_BLOCK
    forced = (lane == 0) | (lane == cur) | (lane == cur - 1)
    valid = lane * SLC_BLOCK <= qpos
    sc = jnp.where(forced, FORCE_SCORE, jnp.where(valid, imp, -FORCE_SCORE))
    sc = jnp.where(lane < ns, sc, -2.0 * FORCE_SCORE)
    rank = jnp.zeros(imp.shape, F32)
    for s in range(ns):
        col = sc[:, s:s + 1]
        beats = (col > sc) | ((col == sc) & (lane > s))
        rank = rank + jnp.where(beats, 1.0, 0.0)
    return jnp.where((rank < float(min(N_SELECT, ns))) & (lane < ns), 1.0, 0.0)


def _overlap_matrix(ncp, cmp_len, width=LANES):
    n = lax.broadcasted_iota(jnp.int32, (ncp, width), 0)
    s = lax.broadcasted_iota(jnp.int32, (ncp, width), 1)
    cstart = n * CMP_STRIDE
    cend = cstart + cmp_len - 1
    bstart = s * SLC_BLOCK
    return jnp.where((cstart < bstart + SLC_BLOCK) & (cend >= bstart), 1.0, 0.0).astype(BF16)


def _cmpatt_body(q_ref, kv_ref, o_ref, sel_ref, *, heads, tq, dk, nc, cmp_len, ns):
    i = pl.program_id(1)
    rows = heads * tq
    ncp = kv_ref.shape[0]
    q = q_ref[...].reshape(rows, dk)
    kv = kv_ref[...]
    kc = kv[:, :dk].astype(BF16)
    vc = kv[:, dk:2 * dk].astype(BF16)
    s = lax.dot_general(q, kc, _NT, preferred_element_type=F32).reshape(heads, tq, ncp)
    qpos = i * tq + lax.broadcasted_iota(jnp.int32, (tq, ncp), 0)
    n = lax.broadcasted_iota(jnp.int32, (tq, ncp), 1)
    mask = ((n * CMP_STRIDE + cmp_len - 1 <= qpos) & (n < nc))[None]
    s = jnp.where(mask, s, NEG)
    m = jnp.max(s, axis=-1, keepdims=True)
    e = jnp.where(mask, jnp.exp(s - m), 0.0)
    p = e / jnp.maximum(jnp.sum(e, axis=-1, keepdims=True), 1e-30)
    o = jnp.dot(p.reshape(rows, ncp).astype(BF16), vc, preferred_element_type=F32)
    o_ref[...] = o.reshape(heads, tq, dk).astype(o_ref.dtype)
    psum = jnp.sum(p, axis=0)
    hi = psum.astype(BF16)
    lo = (psum - hi.astype(F32)).astype(BF16)
    ov = _overlap_matrix(ncp, cmp_len)
    imp = jnp.dot(hi, ov, preferred_element_type=F32) + jnp.dot(lo, ov, preferred_element_type=F32)
    qp = i * tq + lax.broadcasted_iota(jnp.int32, (tq, 1), 0)
    sel_ref[...] = _select_mask(imp, qp, ns).astype(sel_ref.dtype)


def prompt_cmp_attention(nq, kcvc, *, nb, t, dk, cmp_len):
    heads, m, _ = nq.shape
    tq = ATT_TQ
    nqb = t // tq
    ncp = kcvc.shape[1]
    nc = (t - cmp_len) // CMP_STRIDE + 1
    ns = -(-t // SLC_BLOCK)
    assert ns <= LANES and ncp <= LANES
    qmap = lambda b, i: (0, b * nqb + i, 0)
    return pl.pallas_call(
        functools.partial(_cmpatt_body, heads=heads, tq=tq, dk=dk, nc=nc, cmp_len=cmp_len, ns=ns),
        grid=(nb, nqb),
        in_specs=[pl.BlockSpec((heads, tq, dk), qmap), pl.BlockSpec((None, ncp, LANES), lambda b, i: (b, 0, 0))],
        out_specs=[pl.BlockSpec((heads, tq, dk), qmap), pl.BlockSpec((tq, LANES), lambda b, i: (b * nqb + i, 0))],
        out_shape=[jax.ShapeDtypeStruct((heads, m, dk), BF16), jax.ShapeDtypeStruct((m, LANES), BF16)],
        compiler_params=_cparams("parallel", "parallel"),
        name="prompt_cmp_att",
    )(nq, kcvc)


def _gate_body(oc_ref, os_ref, ow_ref, g_ref, e_ref, o_ref, *, heads, dk):
    sig = jax.nn.sigmoid(g_ref[...])
    hi = sig.astype(BF16)
    lo = (sig - hi.astype(F32)).astype(BF16)
    gates = [jnp.dot(hi, e_ref[i], preferred_element_type=F32) + jnp.dot(lo, e_ref[i], preferred_element_type=F32)
             for i in range(3)]
    for h in range(heads):
        sl = slice(h * dk, (h + 1) * dk)
        o = (gates[0][:, sl] * oc_ref[h].astype(F32) + gates[1][:, sl] * os_ref[h].astype(F32)
             + gates[2][:, sl] * ow_ref[h].astype(F32))
        o_ref[:, sl] = o.astype(o_ref.dtype)


def nsa_gate_combine(o_c, o_s, o_w, z, expand, *, gate_col_block):
    heads, m, dk = o_c.shape
    tm = _pick_tile(m, 640, LANES)
    hmap = lambda i: (0, i, 0)
    return pl.pallas_call(
        functools.partial(_gate_body, heads=heads, dk=dk),
        grid=(m // tm,),
        in_specs=[pl.BlockSpec((heads, tm, dk), hmap)] * 3
        + [pl.BlockSpec((tm, LANES), lambda i: (i, gate_col_block)),
           pl.BlockSpec((3, LANES, heads * dk), lambda i: (0, 0, 0))],
        out_specs=pl.BlockSpec((tm, heads * dk), lambda i: (i, 0)),
        out_shape=jax.ShapeDtypeStruct((m, heads * dk), BF16),
        compiler_params=_cparams("parallel"),
        name="nsa_gate_combine",
    )(o_c, o_s, o_w, z, expand)


def _headmm_body(x_ref, w_ref, o_ref):
    o_ref[...] = jnp.dot(x_ref[...], w_ref[...].astype(BF16), preferred_element_type=F32).astype(o_ref.dtype)


def head_matmul(x, w):
    heads, m, k = x.shape
    n = w.shape[2]
    tm = _pick_tile(m, 1664, LANES)
    return pl.pallas_call(
        _headmm_body,
        grid=(m // tm, heads),
        in_specs=[pl.BlockSpec((None, tm, k), lambda i, h: (h, i, 0)),
                  pl.BlockSpec((None, k, n), lambda i, h: (h, 0, 0))],
        out_specs=pl.BlockSpec((tm, n), lambda i, h: (i, h)),
        out_shape=jax.ShapeDtypeStruct((m, heads * n), BF16),
        compiler_params=_cparams("parallel", "parallel"),
        name="head_matmul",
    )(x, w)


def _rms_body(x_ref, g_ref, o_ref):
    x = x_ref[...]
    o_ref[...] = (x * _row_rms(x) * g_ref[...]).astype(o_ref.dtype)


def rms_norm_rows(x, g):
    m, d = x.shape
    tm = _pick_tile(m, 640, LANES)
    return pl.pallas_call(
        _rms_body,
        grid=(m // tm,),
        in_specs=[pl.BlockSpec((tm, d), lambda i: (i, 0)), pl.BlockSpec((1, d), lambda i: (0, 0))],
        out_specs=pl.BlockSpec((tm, d), lambda i: (i, 0)),
        out_shape=jax.ShapeDtypeStruct((m, d), BF16),
        compiler_params=_cparams("parallel"),
        name="rms_norm_rows",
    )(x, g.reshape(1, d))


PAGES_PER_STEP = 8


def _page_specs(layer, width, n_pages, page_size, gp):
    def mk(i):
        return pl.BlockSpec((None, None, page_size, width),
                            lambda b, g, pt: (layer, pt[b * n_pages + g * gp + i], 0, 0))
    return [mk(i) for i in range(gp)]


def _cat_pages(refs):
    return jnp.concatenate([r[...] for r in refs], axis=0) if len(refs) > 1 else refs[0][...]


def _online_update(m_scr, l_scr, acc_scr, s, v_bf, mask=None):
    if mask is not None:
        s = jnp.where(mask, s, NEG)
    m_prev = m_scr[...]
    m_new = jnp.maximum(m_prev, jnp.max(s, axis=-1, keepdims=True))
    alpha = jnp.exp(m_prev - m_new)
    p = jnp.exp(s - m_new)
    if mask is not None:
        p = jnp.where(mask, p, 0.0)
    l_scr[...] = alpha * l_scr[...] + jnp.sum(p, axis=-1, keepdims=True)
    acc_scr[...] = alpha * acc_scr[...] + jnp.dot(p.astype(BF16), v_bf, preferred_element_type=F32)
    m_scr[...] = m_new


def _smla_body(pt_ref, *refs, gp, heads, nope, rope):
    del pt_ref
    lat = refs[:gp]
    krp = refs[gp:2 * gp]
    qa_ref, qr_ref, cs_ref, krs_ref, wt_ref, o_ref, m_scr, l_scr, acc_scr = refs[2 * gp:]
    g = pl.program_id(1)

    @pl.when(g == 0)
    def _():
        m_scr[...] = jnp.full(m_scr.shape, NEG, F32)
        l_scr[...] = jnp.zeros(l_scr.shape, F32)
        acc_scr[...] = jnp.zeros(acc_scr.shape, F32)

    qa, qr, wt = qa_ref[...], qr_ref[...], wt_ref[...]

    def attend(c_bf, kr_bf, mask):
        r = _key_scale(c_bf, wt, heads, nope)
        s = (lax.dot_general(qa, c_bf, _NT, preferred_element_type=F32) * r
             + lax.dot_general(qr, kr_bf, _NT, preferred_element_type=F32))
        _online_update(m_scr, l_scr, acc_scr, s, c_bf, mask)

    attend(_cat_pages(lat).astype(BF16), _cat_pages(krp).astype(BF16), None)

    @pl.when(g == pl.num_programs(1) - 1)
    def _():
        c_new = jnp.broadcast_to(cs_ref[...], (LANES, cs_ref.shape[1])).astype(BF16)
        kr_new = jnp.broadcast_to(krs_ref[...][:, :rope], (LANES, rope)).astype(BF16)
        attend(c_new, kr_new, lax.broadcasted_iota(jnp.int32, (heads, LANES), 1) == 0)
        o_ref[...] = (acc_scr[...] / l_scr[...]).astype(o_ref.dtype)


def sample_mla(page_table_flat, lat_pool, krope_pool, qabs_s, qrope_s, c_new, tb_new, wt_bf, *, layer, heads, nope):
    db, _, kv_rank = qabs_s.shape
    rope = qrope_s.shape[2]
    page_size = lat_pool.shape[2]
    n_pages = page_table_flat.shape[0] // db
    gp = min(PAGES_PER_STEP, n_pages)
    assert n_pages % gp == 0
    per_b = lambda w1, w2: pl.BlockSpec((None, w1, w2), lambda b, g, pt: (b, 0, 0))
    grid_spec = pltpu.PrefetchScalarGridSpec(
        num_scalar_prefetch=1,
        grid=(db, n_pages // gp),
        in_specs=_page_specs(layer, kv_rank, n_pages, page_size, gp) + _page_specs(layer, rope, n_pages, page_size, gp)
        + [per_b(heads, kv_rank), per_b(heads, rope), per_b(1, kv_rank), per_b(1, tb_new.shape[2]),
           pl.BlockSpec((heads * nope, kv_rank), lambda b, g, pt: (0, 0))],
        out_specs=per_b(heads, kv_rank),
        scratch_shapes=[pltpu.VMEM((heads, 1), F32), pltpu.VMEM((heads, 1), F32), pltpu.VMEM((heads, kv_rank), F32)],
    )
    return pl.pallas_call(
        functools.partial(_smla_body, gp=gp, heads=heads, nope=nope, rope=rope),
        grid_spec=grid_spec,
        out_shape=jax.ShapeDtypeStruct((db, heads, kv_rank), BF16),
        compiler_params=_cparams("parallel", "arbitrary"),
        name="sample_mla",
    )(page_table_flat, *([lat_pool] * gp), *([krope_pool] * gp), qabs_s, qrope_s, c_new, tb_new, wt_bf)


def _scmp_body(pt_ref, *refs, gp, heads, dk, hid, page_size, nblk, nc, cmp_len, ns, past_len):
    del pt_ref
    kpg = refs[:gp]
    vpg = refs[gp:2 * gp]
    (q_ref, pos_ref, w1k_ref, w1v_ref, b1_ref, w2k_ref, w2v_ref, gk_ref, cos_ref, sin_ref,
     oc_ref, sel_ref, rk_scr, rv_scr) = refs[2 * gp:]
    g = pl.program_id(1)
    for i in range(gp):
        row0 = pl.multiple_of((g * gp + i) * page_size, page_size)
        rk_scr[pl.ds(row0, page_size), :] = kpg[i][...]
        rv_scr[pl.ds(row0, page_size), :] = vpg[i][...]

    @pl.when(g == pl.num_programs(1) - 1)
    def _():
        def load_kv(j):
            return (rk_scr[pl.ds(j, nblk, stride=CMP_STRIDE), :], rv_scr[pl.ds(j, nblk, stride=CMP_STRIDE), :])
        kv = _compress_core(load_kv, nblk, pos_ref, w1k_ref, w1v_ref, b1_ref, w2k_ref, w2v_ref, gk_ref,
                            cos_ref, sin_ref, dk=dk, hid=hid)
        kc = kv[:, :dk].astype(BF16)
        vc = kv[:, dk:2 * dk].astype(BF16)
        s = lax.dot_general(q_ref[...], kc, _NT, preferred_element_type=F32)
        n = lax.broadcasted_iota(jnp.int32, s.shape, 1)
        mask = (n * CMP_STRIDE + cmp_len - 1 <= past_len) & (n < nc)
        s = jnp.where(mask, s, NEG)
        mx = jnp.max(s, axis=-1, keepdims=True)
        e = jnp.where(mask, jnp.exp(s - mx), 0.0)
        p = e / jnp.maximum(jnp.sum(e, axis=-1, keepdims=True), 1e-30)
        oc_ref[...] = jnp.dot(p.astype(BF16), vc, preferred_element_type=F32)
        psum = jnp.broadcast_to(jnp.sum(p, axis=0, keepdims=True), (8, nblk))
        hi = psum.astype(BF16)
        lo = (psum - hi.astype(F32)).astype(BF16)
        ov = _overlap_matrix(nblk, cmp_len, sel_ref.shape[1])
        imp = jnp.dot(hi, ov, preferred_element_type=F32) + jnp.dot(lo, ov, preferred_element_type=F32)
        sel = _select_mask(imp, jnp.full((8, 1), past_len, jnp.int32), ns)
        sel_ref[...] = sel[0:1, :]


def sample_compress_attend(page_table_flat, kpool, vpool, nq_s, cmpw, *, layer, cmp_len, hid, past_len):
    db, heads, dk = nq_s.shape
    page_size = kpool.shape[2]
    n_pages = page_table_flat.shape[0] // db
    gp = min(PAGES_PER_STEP, n_pages)
    nblk = past_len // CMP_STRIDE
    nc = (past_len + 1 - cmp_len) // CMP_STRIDE + 1
    ns = -(-(past_len + 1) // SLC_BLOCK)
    nsp = -(-ns // LANES) * LANES
    assert n_pages % gp == 0 and nc <= nblk
    z2 = lambda b, g, pt: (0, 0)
    z3 = lambda b, g, pt: (0, 0, 0)
    wspecs = [pl.BlockSpec((cmp_len, 2 * dk), z2), pl.BlockSpec((cmp_len, dk, hid), z3),
              pl.BlockSpec((cmp_len, dk, hid), z3), pl.BlockSpec((2, hid), z2), pl.BlockSpec((hid, LANES), z2),
              pl.BlockSpec((hid, LANES), z2), pl.BlockSpec((1, LANES), z2),
              pl.BlockSpec((nblk, LANES), z2), pl.BlockSpec((nblk, LANES), z2)]
    grid_spec = pltpu.PrefetchScalarGridSpec(
        num_scalar_prefetch=1,
        grid=(db, n_pages // gp),
        in_specs=_page_specs(layer, dk, n_pages, page_size, gp) + _page_specs(layer, dk, n_pages, page_size, gp)
        + [pl.BlockSpec((None, heads, dk), lambda b, g, pt: (b, 0, 0))] + wspecs,
        out_specs=[pl.BlockSpec((None, heads, dk), lambda b, g, pt: (b, 0, 0)),
                   pl.BlockSpec((None, 1, nsp), lambda b, g, pt: (b, 0, 0))],
        scratch_shapes=[pltpu.VMEM((past_len, dk), F32), pltpu.VMEM((past_len, dk), F32)],
    )
    return pl.pallas_call(
        functools.partial(_scmp_body, gp=gp, heads=heads, dk=dk, hid=hid, page_size=page_size, nblk=nblk, nc=nc,
                          cmp_len=cmp_len, ns=ns, past_len=past_len),
        grid_spec=grid_spec,
        out_shape=[jax.ShapeDtypeStruct((db, heads, dk), F32), jax.ShapeDtypeStruct((db, 1, nsp), F32)],
        compiler_params=_cparams("parallel", "arbitrary"),
        name="sample_compress_attend",
    )(page_table_flat, *([kpool] * gp), *([vpool] * gp), nq_s, *cmpw)


def _ssel_body(pt_ref, *refs, gp, heads, dk, page_size, w0):
    del pt_ref
    kpg = refs[:gp]
    vpg = refs[gp:2 * gp]
    (q_ref, sel_ref, ta_ref, td_ref, kwin_ref, vwin_ref, oc_ref, ng_ref, o_ref, m_scr, l_scr, acc_scr) = refs[2 * gp:]
    g = pl.program_id(1)
    gt = gp * page_size

    @pl.when(g == 0)
    def _():
        m_scr[...] = jnp.full(m_scr.shape, NEG, F32)
        l_scr[...] = jnp.zeros(l_scr.shape, F32)
        acc_scr[...] = jnp.zeros(acc_scr.shape, F32)

    q = q_ref[...]
    nsp = sel_ref.shape[1]
    blk = (g * gt + lax.broadcasted_iota(jnp.int32, (nsp, gt), 1)) // SLC_BLOCK
    e = jnp.where(blk == lax.broadcasted_iota(jnp.int32, (nsp, gt), 0), 1.0, 0.0).astype(BF16)
    selm = jnp.dot(jnp.broadcast_to(sel_ref[...], (8, nsp)).astype(BF16), e, preferred_element_type=F32)[0:1, :]
    s = lax.dot_general(q, _cat_pages(kpg).astype(BF16), _NT, preferred_element_type=F32)
    _online_update(m_scr, l_scr, acc_scr, s, _cat_pages(vpg).astype(BF16), jnp.broadcast_to(selm > 0.5, s.shape))

    @pl.when(g == pl.num_programs(1) - 1)
    def _():
        qf = q.astype(F32)
        ta = ta_ref[...]
        td = td_ref[...]
        s_new = jnp.sum(qf * ta[:, :dk], axis=-1, keepdims=True)
        m_new = jnp.maximum(m_scr[...], s_new)
        alpha = jnp.exp(m_scr[...] - m_new)
        p_new = jnp.exp(s_new - m_new)
        o_s = (alpha * acc_scr[...] + p_new * td[:, :dk]) / (alpha * l_scr[...] + p_new)
        sw = lax.dot_general(q, kwin_ref[...].astype(BF16), _NT, preferred_element_type=F32)
        wmask = lax.broadcasted_iota(jnp.int32, sw.shape, 1) >= w0
        sw = jnp.where(wmask, sw, NEG)
        sw_new = jnp.sum(qf * ta[:, dk:2 * dk], axis=-1, keepdims=True)
        mw = jnp.maximum(jnp.max(sw, axis=-1, keepdims=True), sw_new)
        pw = jnp.where(wmask, jnp.exp(sw - mw), 0.0)
        pw_new = jnp.exp(sw_new - mw)
        o_w = ((jnp.dot(pw.astype(BF16), vwin_ref[...].astype(BF16), preferred_element_type=F32)
                + pw_new * td[:, dk:2 * dk]) / (jnp.sum(pw, axis=-1, keepdims=True) + pw_new))
        gate = jax.nn.sigmoid(ng_ref[...])
        o = gate[:, 0:1] * oc_ref[...] + gate[:, 1:2] * o_s + gate[:, 2:3] * o_w
        o_ref[...] = o.astype(o_ref.dtype)


def sample_select_window(page_table_flat, kpool, vpool, kwin, vwin, nq_s, sel, ta_new, td_new, o_c, ngate_s, *, layer):
    db, heads, dk = nq_s.shape
    page_size = kpool.shape[2]
    n_pages = page_table_flat.shape[0] // db
    gp = min(PAGES_PER_STEP, n_pages)
    wbuf = kwin.shape[2]
    w0 = max(0, wbuf + 1 - WINDOW)
    nsp = sel.shape[2]
    per_b = lambda w1, w2: pl.BlockSpec((None, w1, w2), lambda b, g, pt: (b, 0, 0))
    win = pl.BlockSpec((None, None, wbuf, dk), lambda b, g, pt: (layer, b, 0, 0))
    grid_spec = pltpu.PrefetchScalarGridSpec(
        num_scalar_prefetch=1,
        grid=(db, n_pages // gp),
        in_specs=_page_specs(layer, dk, n_pages, page_size, gp) + _page_specs(layer, dk, n_pages, page_size, gp)
        + [per_b(heads, dk), per_b(1, nsp), per_b(1, LANES), per_b(1, LANES), win, win, per_b(heads, dk), per_b(heads, 3)],
        out_specs=per_b(heads, dk),
        scratch_shapes=[pltpu.VMEM((heads, 1), F32), pltpu.VMEM((heads, 1), F32), pltpu.VMEM((heads, dk), F32)],
    )
    return pl.pallas_call(
        functools.partial(_ssel_body, gp=gp, heads=heads, dk=dk, page_size=page_size, w0=w0),
        grid_spec=grid_spec,
        out_shape=jax.ShapeDtypeStruct((db, heads, dk), BF16),
        compiler_params=_cparams("parallel", "arbitrary"),
        name="sample_select_window",
    )(page_table_flat, *([kpool] * gp), *([vpool] * gp), nq_s, sel, ta_new, td_new, kwin, vwin, o_c, ngate_s)


def _rope_tables(pos, group, rot_dim):
    half = rot_dim // 2
    inv = ROPE_THETA ** (-jnp.arange(half, dtype=F32) / half)
    ang = pos.astype(F32)[:, None] * inv[None, :]
    cos, sin = jnp.cos(ang), jnp.sin(ang)
    ones = jnp.ones((pos.shape[0], group - rot_dim), F32)
    c = jnp.concatenate([cos, cos, ones], axis=1)
    s = jnp.concatenate([-sin, sin, 0.0 * ones], axis=1)
    reps = LANES // group
    return jnp.tile(c, (1, reps)), jnp.tile(s, (1, reps))


def kernel(x_prompt, x_sample, cache_mla_latent, cache_mla_krope, cache_nsa_kcmp, cache_nsa_vcmp,
           cache_nsa_kslc, cache_nsa_vslc, state_nsa_kwin, state_nsa_vwin, page_table,
           g_attn, w_in, g_qa, w_qb, g_kva, g_qn, g_qr, g_kn, g_kr, w_kb, w_vb, w_mla_o,
           g_nq, g_nkc, g_nks, g_nkw, cmp_pos, cmp_w1, cmp_b1, cmp_w2, w_nsa_o,
           w_out, g_mlp, w_up, w_down):
    nb, t_p, d_model = x_prompt.shape
    db, t_s, _ = x_sample.shape
    assert t_s == 1
    depth = w_in.shape[0]
    q_rank = g_qa.shape[1]
    kv_rank = g_kva.shape[1]
    nope = g_qn.shape[1]
    rope = g_qr.shape[1]
    heads = w_kb.shape[1]
    mla_v = w_vb.shape[3]
    dk = g_nq.shape[1]
    nsa_heads = w_nsa_o.shape[1] // dk
    cmp_len = cmp_pos.shape[2]
    hid = cmp_w1.shape[3]
    page_size = cache_mla_latent.shape[2]
    past_len = page_table.shape[1] * page_size
    wkeep_p = min(WINDOW, t_p)
    mla_scale = (nope + rope) ** -0.5
    assert cmp_len == 2 * CMP_STRIDE and 2 * dk == LANES and 2 * rope == LANES and nope == LANES

    n_p = nb * t_p
    n_s = db * t_s
    m = n_p + n_s
    nqw = nsa_heads * dk

    o_nq = q_rank + kv_rank
    o_ta = o_nq + nqw
    o_tb = o_ta + LANES
    o_tc = o_tb + LANES
    o_td = o_tc + LANES
    o_te = o_td + LANES
    tn_g = min(512, d_model)
    o_ga = -(-(o_te + LANES) // tn_g) * tn_g
    o_gb = o_ga + d_model
    assert 3 * nsa_heads <= LANES and o_gb % tn_g == 0
    widths = [q_rank, kv_rank, rope, nqw] + [dk] * 6 + [3 * nsa_heads, d_model, d_model]
    off = np.cumsum([0] + widths)
    seg = lambda w, i: w[:, int(off[i]):int(off[i + 1])]

    pos_rows = jnp.concatenate([jnp.tile(jnp.arange(t_p, dtype=jnp.int32), nb), jnp.full((n_s,), past_len, jnp.int32)])
    cn, sn = _rope_tables(pos_rows, dk, dk // 4)
    cm, sm = _rope_tables(pos_rows, rope, rope)
    nblk_p = t_p // CMP_STRIDE
    cend_p = jnp.arange(nblk_p, dtype=jnp.int32) * CMP_STRIDE + cmp_len - 1
    cc_p, sc_p = _rope_tables(cend_p, dk, dk // 4)
    lane = jnp.arange(LANES)
    cc_p = jnp.where(lane[None, :] < dk, cc_p, 1.0)
    sc_p = jnp.where(lane[None, :] < dk, sc_p, 0.0)
    cend_s = jnp.arange(past_len // CMP_STRIDE, dtype=jnp.int32) * CMP_STRIDE + cmp_len - 1
    cc_s, sc_s = _rope_tables(cend_s, dk, dk // 4)
    cc_s = jnp.where(lane[None, :] < dk, cc_s, 1.0)
    sc_s = jnp.where(lane[None, :] < dk, sc_s, 0.0)
    pt_flat = page_table.reshape(-1)

    gi = jnp.arange(LANES)[:, None]
    hd = jnp.arange(nqw)[None, :] // dk
    expand = jnp.stack([jnp.where(gi == 3 * hd + i, 1.0, 0.0) for i in range(3)]).astype(BF16)

    x = jnp.concatenate([x_prompt.reshape(n_p, d_model), x_sample.reshape(n_s, d_model)], axis=0)
    p_rows, s_rows = [], []
    tk_mla = _pick_tile(t_p, 512, LANES)

    for l in range(depth):
        wl = w_in[l]
        zc = jnp.zeros((d_model, dk), F32)
        w_perm = jnp.concatenate(
            [seg(wl, 0), seg(wl, 1), seg(wl, 3), seg(wl, 6), seg(wl, 8), seg(wl, 2), zc, seg(wl, 4), seg(wl, 5),
             seg(wl, 7), seg(wl, 9), seg(wl, 10), jnp.zeros((d_model, o_ga - o_te - 3 * nsa_heads), F32),
             seg(wl, 11), seg(wl, 12)], axis=1)
        wq = w_qb[l].reshape(q_rank, heads, nope + rope)
        wq_perm = jnp.concatenate([wq[..., :nope].reshape(q_rank, heads * nope),
                                   wq[..., nope:].reshape(q_rank, heads * rope)], axis=1)
        wkb_t = jnp.transpose(w_kb[l], (0, 2, 1)).reshape(heads * nope, kv_rank).astype(BF16)

        h = rms_norm_rows(x, g_attn[l])
        z = matmul(h, w_perm)
        g_a = jnp.concatenate([g_nks[l], g_nkw[l]]).reshape(1, LANES)
        g_b = jnp.concatenate([g_kr[l], jnp.ones((LANES - rope,), F32)]).reshape(1, LANES)
        qa, c, nq, ta, tb = post_in_proj(
            z, g_qa[l].reshape(1, -1), g_kva[l].reshape(1, -1), jnp.tile(g_nq[l], nsa_heads).reshape(1, -1),
            g_a, g_b, cn, sn, cm, sm, q_rank=q_rank, kv_rank=kv_rank, nsa_heads=nsa_heads, dk=dk, mla_rope=rope)
        q = matmul(qa, wq_perm, tk=q_rank)
        q_abs, q_rope = post_q_proj(
            q, (g_qn[l] * g_kn[l]).reshape(1, -1), jnp.tile(g_qr[l], LANES // rope).reshape(1, -1), w_kb[l], cm, sm,
            heads=heads, nope=nope, rope=rope, kv_rank=kv_rank, scale=mla_scale)

        r_p = prompt_key_scale(c, wkb_t, nb=nb, t=t_p, heads=heads, nope=nope, tk=tk_mla)
        o_lat = prompt_attention(q_abs, c, c, mode="causal", nb=nb, t=t_p, tk=tk_mla, d1=kv_rank, dv=kv_rank,
                                 q2=q_rope, k2=tb, d2=rope, r=r_p)
        cmpw = (jnp.concatenate([cmp_pos[l, 0], cmp_pos[l, 1]], axis=1),
                cmp_w1[l, 0].reshape(cmp_len, dk, hid), cmp_w1[l, 1].reshape(cmp_len, dk, hid), cmp_b1[l],
                jnp.pad(cmp_w2[l, 0], ((0, 0), (0, LANES - dk))), jnp.pad(cmp_w2[l, 1], ((0, 0), (LANES - dk, 0))),
                jnp.concatenate([g_nkc[l], jnp.ones((LANES - dk,), F32)]).reshape(1, LANES))
        kcvc_p = compress_prompt(z, cmpw + (cc_p, sc_p), nb=nb, t=t_p, col_block=o_tc // LANES, dk=dk, hid=hid,
                                 cmp_len=cmp_len)
        o_c, sel = prompt_cmp_attention(nq, kcvc_p, nb=nb, t=t_p, dk=dk, cmp_len=cmp_len)
        o_s = prompt_attention(nq, ta, z, mode="select", nb=nb, t=t_p, tk=tk_mla, d1=dk, dv=dk, sel=sel,
                               v_col_block=o_td // LANES)
        o_w = prompt_attention(nq, ta, z, mode="window", nb=nb, t=t_p, tk=ATT_TQ, d1=dk, dv=dk, k1_off=dk, v_off=dk,
                               v_col_block=o_td // LANES)

        sl_s = slice(n_p, m)
        c_s = c[sl_s][:, None, :]
        kr_s = tb[sl_s, :rope][:, None, :]
        kcr_s = z[sl_s, o_tc:o_tc + dk][:, None, :]
        vcr_s = z[sl_s, o_tc + dk:o_tc + 2 * dk][:, None, :]
        ks_s = ta[sl_s, :dk][:, None, :]
        kw_s = ta[sl_s, dk:][:, None, :]
        vs_s = z[sl_s, o_td:o_td + dk][:, None, :]
        vw_s = z[sl_s, o_td + dk:o_td + 2 * dk][:, None, :]
        ng_s = z[sl_s, o_te:o_te + 3 * nsa_heads].reshape(n_s, nsa_heads, 3)
        qabs_s = jnp.transpose(q_abs[:, sl_s], (1, 0, 2))
        qrope_s = jnp.transpose(q_rope[:, sl_s], (1, 0, 2))
        nq_s = jnp.transpose(nq[:, sl_s], (1, 0, 2))
        olat_s = sample_mla(pt_flat, cache_mla_latent, cache_mla_krope, qabs_s, qrope_s, c_s, tb[sl_s][:, None, :],
                            wkb_t, layer=l, heads=heads, nope=nope)
        oc_s, sel_s = sample_compress_attend(pt_flat, cache_nsa_kcmp, cache_nsa_vcmp, nq_s, cmpw + (cc_s, sc_s),
                                             layer=l, cmp_len=cmp_len, hid=hid, past_len=past_len)
        onsa_s = sample_select_window(pt_flat, cache_nsa_kslc, cache_nsa_vslc, state_nsa_kwin, state_nsa_vwin, nq_s,
                                      sel_s, ta[sl_s][:, None, :], z[sl_s, o_td:o_td + LANES][:, None, :], oc_s, ng_s,
                                      layer=l)
        kw_all = jnp.concatenate([state_nsa_kwin[l], kw_s], axis=1)
        vw_all = jnp.concatenate([state_nsa_vwin[l], vw_s], axis=1)

        o_lat = lax.dynamic_update_slice(o_lat, jnp.transpose(olat_s, (1, 0, 2)), (0, n_p, 0))
        o_mla = head_matmul(o_lat, w_vb[l])
        o_nsa = nsa_gate_combine(o_c, o_s, o_w, z, expand, gate_col_block=o_te // LANES)
        o_nsa = lax.dynamic_update_slice(o_nsa, onsa_s.reshape(n_s, nqw), (n_p, 0))
        y1 = matmul(o_mla, w_mla_o[l], epilogue="gate", extras=(z,), extra_col_blocks=(o_ga // tn_g,), tn=tn_g)
        y = matmul(o_nsa, w_nsa_o[l], epilogue="gate_add", extras=(z, y1), extra_col_blocks=(o_gb // tn_g, 0),
                   out_dtype=BF16, tn=tn_g)
        x = matmul(y, w_out[l], epilogue="add", extras=(x,), tn=512)
        h2 = rms_norm_rows(x, g_mlp[l])
        u = matmul(h2, w_up[l], epilogue="relu2", out_dtype=BF16)
        x = matmul(u, w_down[l], epilogue="add", extras=(x,), tn=512)

        pr = lambda a: a[:n_p].reshape(nb, t_p, -1)
        kw_p = pr(ta[:, dk:])
        vw_p = pr(z[:, o_td + dk:o_td + 2 * dk])
        p_rows.append((pr(c), pr(tb[:, :rope]), pr(z[:, o_tc:o_tc + dk]), pr(z[:, o_tc + dk:o_tc + 2 * dk]),
                       pr(ta[:, :dk]), pr(z[:, o_td:o_td + dk]), kw_p[:, t_p - wkeep_p:], vw_p[:, t_p - wkeep_p:]))
        s_rows.append((c_s, kr_s, kcr_s, vcr_s, ks_s, vs_s, kw_all[:, t_s:], vw_all[:, t_s:]))

    xp = x[:n_p].reshape(nb, t_p, d_model)
    xs = x[n_p:].reshape(db, t_s, d_model)
    (lat_p, krope_p, kcmp_p, vcmp_p, kslc_p, vslc_p, kwin_p, vwin_p) = [jnp.stack(a, axis=0) for a in zip(*p_rows)]
    (lat_s, krope_s, kcmp_s, vcmp_s, kslc_s, vslc_s, kwin_s, vwin_s) = [jnp.stack(a, axis=0) for a in zip(*s_rows)]
    return (xp, xs, lat_p, lat_s, krope_p, krope_s, kcmp_p, kcmp_s, vcmp_p, vcmp_s,
            kslc_p, kslc_s, vslc_p, vslc_s, kwin_p, kwin_s, vwin_p, vwin_s)
```

```python
import functools

import jax
import jax.numpy as jnp
import numpy as np
from jax import lax
from jax.experimental import pallas as pl
from jax.experimental.pallas import tpu as pltpu

CMP_STRIDE = 16
SLC_BLOCK = 64
N_SELECT = 16
WINDOW = 512
ROPE_THETA = 500000.0
EPS = 1e-6
NEG = -1e30
FORCE_SCORE = 1e9

F32 = jnp.float32
BF16 = jnp.bfloat16
LANES = 128
ATT_TQ = 128
KEY_SCALE_CHUNK = 256

VMEM_LIMIT_BYTES = 56 * 1024 * 1024

_NT = (((1,), (1,)), ((), ()))


def _cparams(*sem):
    return pltpu.CompilerParams(dimension_semantics=sem, vmem_limit_bytes=VMEM_LIMIT_BYTES)


def _pick_tile(dim, target, quantum):
    best = None
    t = quantum
    while t <= min(dim, target):
        if dim % t == 0:
            best = t
        t += quantum
    return best if best is not None else dim


def _mm_body(*refs, nk, epilogue, n_extra):
    a_ref, b_ref = refs[0], refs[1]
    extras = refs[2:2 + n_extra]
    o_ref = refs[2 + n_extra]
    acc_ref = refs[3 + n_extra] if nk > 1 else None

    part = jnp.dot(a_ref[...].astype(BF16), b_ref[...].astype(BF16), preferred_element_type=F32)

    def finish(acc):
        if epilogue == "none":
            r = acc
        elif epilogue == "relu2":
            r = jnp.square(jnp.maximum(acc, 0.0))
        elif epilogue == "add":
            r = acc + extras[0][...]
        elif epilogue == "gate":
            r = jax.nn.sigmoid(extras[0][...]) * acc
        elif epilogue == "gate_add":
            r = jax.nn.sigmoid(extras[0][...]) * acc + extras[1][...]
        else:
            raise ValueError(epilogue)
        o_ref[...] = r.astype(o_ref.dtype)

    if nk == 1:
        finish(part)
        return

    k = pl.program_id(2)

    @pl.when(k == 0)
    def _():
        acc_ref[...] = part

    @pl.when(k > 0)
    def _():
        acc_ref[...] += part

    @pl.when(k == nk - 1)
    def _():
        finish(acc_ref[...])


MM_TM, MM_TN, MM_TK = 1664, 256, 4096


def matmul(a, b, *, layer=None, epilogue="none", extras=(), extra_col_blocks=None, out_dtype=F32, tm=MM_TM, tn=MM_TN,
           tk=MM_TK):
    m, kdim = a.shape
    kdim2, n = b.shape[-2:]
    assert kdim == kdim2 and (b.ndim == 2) == (layer is None)
    tm = _pick_tile(m, tm, LANES)
    tk = _pick_tile(kdim, tk, LANES)
    tn = min(tn, n)
    nk = kdim // tk
    grid = (m // tm, pl.cdiv(n, tn), nk)
    if layer is None:
        b_spec = pl.BlockSpec((tk, tn), lambda i, j, k: (k, j))
    else:
        b_spec = pl.BlockSpec((None, tk, tn), lambda i, j, k: (layer, k, j))
    in_specs = [pl.BlockSpec((tm, tk), lambda i, j, k: (i, k)), b_spec]
    offs = extra_col_blocks or (0,) * len(extras)
    for o in offs:
        in_specs.append(pl.BlockSpec((tm, tn), lambda i, j, k, o=o: (i, j + o)))
    scratch = [pltpu.VMEM((tm, tn), F32)] if nk > 1 else []
    return pl.pallas_call(
        functools.partial(_mm_body, nk=nk, epilogue=epilogue, n_extra=len(extras)),
        grid=grid,
        in_specs=in_specs,
        out_specs=pl.BlockSpec((tm, tn), lambda i, j, k: (i, j)),
        out_shape=jax.ShapeDtypeStruct((m, n), out_dtype),
        scratch_shapes=scratch,
        compiler_params=_cparams("parallel", "parallel", "arbitrary"),
        name="mm_" + epilogue,
    )(a, b, *extras)


def _group_mean_sq(x, group):
    w = x.shape[1]
    xx = x * x
    hi = xx.astype(BF16)
    lo = (xx - hi.astype(F32)).astype(BF16)
    r = lax.broadcasted_iota(jnp.int32, (LANES, LANES), 0) // group
    c = lax.broadcasted_iota(jnp.int32, (LANES, LANES), 1) // group
    bd = jnp.where(r == c, 1.0, 0.0).astype(BF16)
    outs = []
    for j in range(w // LANES):
        sl = slice(j * LANES, (j + 1) * LANES)
        outs.append(jnp.dot(hi[:, sl], bd, preferred_element_type=F32)
                    + jnp.dot(lo[:, sl], bd, preferred_element_type=F32))
    ms = outs[0] if len(outs) == 1 else jnp.concatenate(outs, axis=1)
    return ms * (1.0 / group)


def _rope128(y, cos, sin, half, group):
    lane = lax.broadcasted_iota(jnp.int32, y.shape, 1) % group
    partner = jnp.where(lane < half, pltpu.roll(y, LANES - half, 1), pltpu.roll(y, half, 1))
    return y * cos + partner * sin


def _row_rms(x):
    return lax.rsqrt(jnp.mean(x * x, axis=-1, keepdims=True) + EPS)


def _p1_body(z_ref, gqa_ref, gkva_ref, gnq_ref, ga_ref, gb_ref, cn_ref, sn_ref, cm_ref, sm_ref,
             qa_ref, c_ref, nq_ref, ta_ref, tb_ref, *, q_rank, kv_rank, nqw, dk, nsa_half, mla_half, nsa_scale):
    o1 = q_rank + kv_rank
    x = z_ref[:, 0:q_rank]
    qa_ref[...] = (x * _row_rms(x) * gqa_ref[...]).astype(qa_ref.dtype)
    x = z_ref[:, q_rank:o1]
    c_ref[...] = x * _row_rms(x) * gkva_ref[...]
    cn, sn = cn_ref[...], sn_ref[...]
    heads_per_tile = LANES // dk
    for j in range(nqw // LANES):
        x = z_ref[:, o1 + j * LANES:o1 + (j + 1) * LANES]
        y = x * lax.rsqrt(_group_mean_sq(x, dk) + EPS) * gnq_ref[:, j * LANES:(j + 1) * LANES]
        y = _rope128(y, cn, sn, nsa_half, dk) * nsa_scale
        for hh in range(heads_per_tile):
            nq_ref[j * heads_per_tile + hh] = y[:, hh * dk:(hh + 1) * dk].astype(nq_ref.dtype)
    o2 = o1 + nqw
    x = z_ref[:, o2:o2 + LANES]
    y = x * lax.rsqrt(_group_mean_sq(x, dk) + EPS) * ga_ref[...]
    ta_ref[...] = _rope128(y, cn, sn, nsa_half, dk)
    x = z_ref[:, o2 + LANES:o2 + 2 * LANES]
    y = x * lax.rsqrt(_group_mean_sq(x, 2 * mla_half) + EPS) * gb_ref[...]
    tb_ref[...] = _rope128(y, cm_ref[...], sm_ref[...], mla_half, 2 * mla_half)


def post_in_proj(z, gqa, gkva, gnq_t, g_a, g_b, cn, sn, cm, sm, *, q_rank, kv_rank, nsa_heads, dk, mla_rope):
    m = z.shape[0]
    nqw = nsa_heads * dk
    win = q_rank + kv_rank + nqw + 2 * LANES
    tm = _pick_tile(m, 640, LANES)
    row = lambda i: (i, 0)
    cst = lambda i: (0, 0)
    return pl.pallas_call(
        functools.partial(_p1_body, q_rank=q_rank, kv_rank=kv_rank, nqw=nqw, dk=dk, nsa_half=dk // 8,
                          mla_half=mla_rope // 2, nsa_scale=dk ** -0.5),
        grid=(m // tm,),
        in_specs=[pl.BlockSpec((tm, win), row),
                  pl.BlockSpec((1, q_rank), cst), pl.BlockSpec((1, kv_rank), cst), pl.BlockSpec((1, nqw), cst),
                  pl.BlockSpec((1, LANES), cst), pl.BlockSpec((1, LANES), cst),
                  pl.BlockSpec((tm, LANES), row), pl.BlockSpec((tm, LANES), row),
                  pl.BlockSpec((tm, LANES), row), pl.BlockSpec((tm, LANES), row)],
        out_specs=[pl.BlockSpec((tm, q_rank), row), pl.BlockSpec((tm, kv_rank), row),
                   pl.BlockSpec((nsa_heads, tm, dk), lambda i: (0, i, 0)),
                   pl.BlockSpec((tm, LANES), row), pl.BlockSpec((tm, LANES), row)],
        out_shape=[jax.ShapeDtypeStruct((m, q_rank), BF16), jax.ShapeDtypeStruct((m, kv_rank), F32),
                   jax.ShapeDtypeStruct((nsa_heads, m, dk), BF16),
                   jax.ShapeDtypeStruct((m, LANES), F32), jax.ShapeDtypeStruct((m, LANES), F32)],
        compiler_params=_cparams("parallel"),
        name="post_in_proj",
    )(z, gqa, gkva, gnq_t, g_a, g_b, cn, sn, cm, sm)


def _p2_body(q_ref, gn_ref, gr_ref, wkb_ref, cm_ref, sm_ref, qabs_ref, qr_ref, *, heads, nope, rope, scale):
    for h in range(heads):
        x = q_ref[:, h * nope:(h + 1) * nope]
        y = x * lax.rsqrt(_group_mean_sq(x, nope) + EPS) * gn_ref[...] * scale
        qa = lax.dot_general(y.astype(BF16), wkb_ref[h].astype(BF16), _NT, preferred_element_type=F32)
        qabs_ref[h] = qa.astype(qabs_ref.dtype)
    o = heads * nope
    per_tile = LANES // rope
    cm, sm = cm_ref[...], sm_ref[...]
    for j in range(heads * rope // LANES):
        x = q_ref[:, o + j * LANES:o + (j + 1) * LANES]
        y = x * lax.rsqrt(_group_mean_sq(x, rope) + EPS) * gr_ref[...]
        y = _rope128(y, cm, sm, rope // 2, rope) * scale
        for hh in range(per_tile):
            qr_ref[j * per_tile + hh] = y[:, hh * rope:(hh + 1) * rope].astype(qr_ref.dtype)


def post_q_proj(q, gn, gr, w_kb, cm, sm, *, layer, heads, nope, rope, kv_rank, scale):
    m = q.shape[0]
    tm = _pick_tile(m, 640, LANES)
    row = lambda i: (i, 0)
    cst = lambda i: (0, 0)
    return pl.pallas_call(
        functools.partial(_p2_body, heads=heads, nope=nope, rope=rope, scale=scale),
        grid=(m // tm,),
        in_specs=[pl.BlockSpec((tm, heads * (nope + rope)), row),
                  pl.BlockSpec((1, nope), cst), pl.BlockSpec((1, LANES), cst),
                  pl.BlockSpec((None, heads, kv_rank, nope), lambda i: (layer, 0, 0, 0)),
                  pl.BlockSpec((tm, LANES), row), pl.BlockSpec((tm, LANES), row)],
        out_specs=[pl.BlockSpec((heads, tm, kv_rank), lambda i: (0, i, 0)),
                   pl.BlockSpec((heads, tm, rope), lambda i: (0, i, 0))],
        out_shape=[jax.ShapeDtypeStruct((heads, m, kv_rank), BF16), jax.ShapeDtypeStruct((heads, m, rope), BF16)],
        compiler_params=_cparams("parallel"),
        name="post_q_proj",
    )(q, gn, gr, w_kb, cm, sm)


def _key_scale(c_bf, wt_bf, heads, nope):
    n = c_bf.shape[0]
    step = min(n, KEY_SCALE_CHUNK)
    parts = []
    for t0 in range(0, n, step):
        kt = lax.dot_general(wt_bf, c_bf[t0:t0 + step], _NT, preferred_element_type=F32)
        kt = kt * kt
        parts.append(jnp.sum(kt.reshape(heads, nope, step), axis=1))
    ss = parts[0] if len(parts) == 1 else jnp.concatenate(parts, axis=1)
    return lax.rsqrt(ss * (1.0 / nope) + EPS)


def _ks_body(c_ref, wt_ref, r_ref, *, heads, nope):
    r_ref[...] = _key_scale(c_ref[...].astype(BF16), wt_ref[...], heads, nope)


def prompt_key_scale(c, wt_bf, *, nb, t, heads, nope, tk):
    kv_rank = c.shape[1]
    nch = t // tk
    return pl.pallas_call(
        functools.partial(_ks_body, heads=heads, nope=nope),
        grid=(nb, nch),
        in_specs=[pl.BlockSpec((tk, kv_rank), lambda b, j: (b * nch + j, 0)),
                  pl.BlockSpec((heads * nope, kv_rank), lambda b, j: (0, 0))],
        out_specs=pl.BlockSpec((None, None, heads, tk), lambda b, j: (b, j, 0, 0)),
        out_shape=jax.ShapeDtypeStruct((nb, nch, heads, tk), F32),
        compiler_params=_cparams("parallel", "parallel"),
        name="prompt_key_scale",
    )(c, wt_bf)


def _att_body(*refs, mode, heads, tq, tk, d1, d2, dv, k1_off, k2_off, v_off, has_r):
    it = iter(refs)
    q1_ref = next(it)
    q2_ref = next(it) if d2 else None
    k1_ref = next(it)
    k2_ref = next(it) if d2 else None
    v_ref = next(it)
    r_ref = next(it) if has_r else None
    sel_ref = next(it) if mode == "select" else None
    o_ref, m_scr, l_scr, acc_scr = next(it), next(it), next(it), next(it)

    i = pl.program_id(1)
    q0 = i * tq
    rows = heads * tq
    q1 = q1_ref[...].reshape(rows, d1)
    q2 = q2_ref[...].reshape(rows, d2) if d2 else None

    m_scr[...] = jnp.full(m_scr.shape, NEG, F32)
    l_scr[...] = jnp.zeros(l_scr.shape, F32)
    acc_scr[...] = jnp.zeros(acc_scr.shape, F32)

    hi = (q0 + tq + tk - 1) // tk

    def chunk(kci, carry):
        ks = pl.multiple_of(kci * tk, tk)
        k1c = k1_ref[pl.ds(ks, tk), k1_off:k1_off + d1].astype(BF16)
        s = lax.dot_general(q1, k1c, _NT, preferred_element_type=F32).reshape(heads, tq, tk)
        if has_r:
            s = s * r_ref[kci][:, None, :]
        if d2:
            k2c = k2_ref[pl.ds(ks, tk), k2_off:k2_off + d2].astype(BF16)
            s = s + lax.dot_general(q2, k2c, _NT, preferred_element_type=F32).reshape(heads, tq, tk)
        qpos = q0 + lax.broadcasted_iota(jnp.int32, (tq, tk), 0)
        kpos = ks + lax.broadcasted_iota(jnp.int32, (tq, tk), 1)
        dist = qpos - kpos
        if mode == "select":
            blk = (ks + lax.broadcasted_iota(jnp.int32, (LANES, tk), 1)) // SLC_BLOCK
            e = jnp.where(blk == lax.broadcasted_iota(jnp.int32, (LANES, tk), 0), 1.0, 0.0).astype(BF16)
            selm = jnp.dot(sel_ref[...], e, preferred_element_type=F32)
            mask = jnp.where(dist >= 0, selm, 0.0) > 0.5
        else:
            mask = dist >= 0
        mask = mask[None]
        s = jnp.where(mask, s, NEG)
        m_prev = m_scr[...]
        m_new = jnp.maximum(m_prev, jnp.max(s, axis=-1, keepdims=True))
        alpha = jnp.exp(m_prev - m_new)
        p = jnp.where(mask, jnp.exp(s - m_new), 0.0)
        l_scr[...] = alpha * l_scr[...] + jnp.sum(p, axis=-1, keepdims=True)
        vc = v_ref[pl.ds(ks, tk), v_off:v_off + dv].astype(BF16)
        pv = jnp.dot(p.reshape(rows, tk).astype(BF16), vc, preferred_element_type=F32)
        acc_scr[...] = alpha * acc_scr[...] + pv.reshape(heads, tq, dv)
        m_scr[...] = m_new
        return carry

    lax.fori_loop(0, hi, chunk, 0)
    o_ref[...] = (acc_scr[...] / l_scr[...]).astype(o_ref.dtype)


def prompt_attention(q1, k1, v, *, mode, nb, t, tk, d1, dv, k1_off=0, v_off=0, q2=None, k2=None, d2=0, k2_off=0,
                     r=None, sel=None, v_col_block=0):
    heads = q1.shape[0]
    tq = ATT_TQ
    nq = t // tq
    qmap = lambda b, i: (0, b * nq + i, 0)
    kmap = lambda b, i: (b, 0)
    args, specs = [q1], [pl.BlockSpec((heads, tq, d1), qmap)]
    if d2:
        args.append(q2)
        specs.append(pl.BlockSpec((heads, tq, d2), qmap))
    args.append(k1)
    specs.append(pl.BlockSpec((t, k1.shape[1]), kmap))
    if d2:
        args.append(k2)
        specs.append(pl.BlockSpec((t, k2.shape[1]), kmap))
    args.append(v)
    vw = LANES if v.shape[1] > max(LANES, dv) else v.shape[1]
    specs.append(pl.BlockSpec((t, vw), lambda b, i: (b, v_col_block)))
    if r is not None:
        args.append(r)
        specs.append(pl.BlockSpec((None, t // tk, heads, tk), lambda b, i: (b, 0, 0, 0)))
    if sel is not None:
        args.append(sel)
        specs.append(pl.BlockSpec((tq, LANES), lambda b, i: (b * nq + i, 0)))
    return pl.pallas_call(
        functools.partial(_att_body, mode=mode, heads=heads, tq=tq, tk=tk, d1=d1, d2=d2, dv=dv,
                          k1_off=k1_off, k2_off=k2_off, v_off=v_off, has_r=r is not None),
        grid=(nb, nq),
        in_specs=specs,
        out_specs=pl.BlockSpec((heads, tq, dv), qmap),
        out_shape=jax.ShapeDtypeStruct((heads, q1.shape[1], dv), BF16),
        scratch_shapes=[pltpu.VMEM((heads, tq, 1), F32), pltpu.VMEM((heads, tq, 1), F32),
                        pltpu.VMEM((heads, tq, dv), F32)],
        compiler_params=_cparams("parallel", "parallel"),
        name="prompt_att_" + mode,
    )(*args)


def _win_body(q_ref, k_ref, v_ref, o_ref, *, heads, tq, wk, d, k_off, v_off):
    q0 = pl.program_id(1) * tq
    rows = heads * tq
    q = q_ref[...].reshape(rows, d)
    start = pl.multiple_of(jnp.maximum(q0 + tq - wk, 0), tq)
    kc = k_ref[pl.ds(start, wk), k_off:k_off + d].astype(BF16)
    s = lax.dot_general(q, kc, _NT, preferred_element_type=F32).reshape(heads, tq, wk)
    dist = (q0 - start) + lax.broadcasted_iota(jnp.int32, (tq, wk), 0) - lax.broadcasted_iota(jnp.int32, (tq, wk), 1)
    mask = ((dist >= 0) & (dist < WINDOW))[None]
    s = jnp.where(mask, s, NEG)
    p = jnp.where(mask, jnp.exp(s - jnp.max(s, axis=-1, keepdims=True)), 0.0)
    l = jnp.sum(p, axis=-1, keepdims=True)
    vc = v_ref[pl.ds(start, wk), v_off:v_off + d].astype(BF16)
    o = jnp.dot(p.reshape(rows, wk).astype(BF16), vc, preferred_element_type=F32).reshape(heads, tq, d)
    o_ref[...] = (o / l).astype(o_ref.dtype)


def prompt_window_attention(q, k, v, *, nb, t, d, k_off, v_off, v_col_block):
    heads = q.shape[0]
    tq = ATT_TQ
    nq = t // tq
    wk = min(WINDOW + tq, t)
    assert wk % tq == 0
    qmap = lambda b, i: (0, b * nq + i, 0)
    return pl.pallas_call(
        functools.partial(_win_body, heads=heads, tq=tq, wk=wk, d=d, k_off=k_off, v_off=v_off),
        grid=(nb, nq),
        in_specs=[pl.BlockSpec((heads, tq, d), qmap), pl.BlockSpec((t, k.shape[1]), lambda b, i: (b, 0)),
                  pl.BlockSpec((t, LANES), lambda b, i: (b, v_col_block))],
        out_specs=pl.BlockSpec((heads, tq, d), qmap),
        out_shape=jax.ShapeDtypeStruct((heads, q.shape[1], d), BF16),
        compiler_params=_cparams("parallel", "parallel"),
        name="prompt_att_window",
    )(q, k, v)


def _compress_core(load_kv, nblk, pos_ref, w1k_ref, w1v_ref, b1_ref, w2k_ref, w2v_ref, g_ref, cos_ref, sin_ref, *, dk, hid):
    half = CMP_STRIDE
    acc = [jnp.zeros((nblk, hid), F32) for _ in range(4)]
    for j in range(half):
        kr, vr = load_kv(j)
        for part, (rows_, w_ref, col) in enumerate(((kr, w1k_ref, 0), (kr, w1k_ref, 0), (vr, w1v_ref, dk), (vr, w1v_ref, dk))):
            jj = j + (half if part % 2 else 0)
            xin = (rows_ + pos_ref[jj:jj + 1, col:col + dk]).astype(BF16)
            acc[part] = acc[part] + jnp.dot(xin, w_ref[jj].astype(BF16), preferred_element_type=F32)
    hk = acc[0] + pltpu.roll(acc[1], nblk - 1, 0) + b1_ref[0:1, :]
    hv = acc[2] + pltpu.roll(acc[3], nblk - 1, 0) + b1_ref[1:2, :]
    kv = (jnp.dot(jax.nn.gelu(hk).astype(BF16), w2k_ref[...].astype(BF16), preferred_element_type=F32)
          + jnp.dot(jax.nn.gelu(hv).astype(BF16), w2v_ref[...].astype(BF16), preferred_element_type=F32))
    lane = lax.broadcasted_iota(jnp.int32, kv.shape, 1)
    y = jnp.where(lane < dk, kv * lax.rsqrt(_group_mean_sq(kv, dk) + EPS) * g_ref[...], kv)
    return _rope128(y, cos_ref[...], sin_ref[...], dk // 8, dk)


def _cmp_prompt_body(rows_ref, pos_ref, w1k_ref, w1v_ref, b1_ref, w2k_ref, w2v_ref, g_ref, cos_ref, sin_ref, o_ref,
                     *, nblk, dk, hid):
    def load_kv(j):
        rows_ = rows_ref[pl.ds(j, nblk, stride=CMP_STRIDE), :]
        return rows_[:, :dk], rows_[:, dk:2 * dk]
    o_ref[...] = _compress_core(load_kv, nblk, pos_ref, w1k_ref, w1v_ref, b1_ref, w2k_ref, w2v_ref, g_ref,
                                cos_ref, sin_ref, dk=dk, hid=hid)


def _cmp_weight_specs(cmp_len, dk, hid, nblk, nidx):
    z2 = (lambda *a: (0, 0))
    z3 = (lambda *a: (0, 0, 0))
    return [pl.BlockSpec((cmp_len, 2 * dk), z2),
            pl.BlockSpec((cmp_len, dk, hid), z3), pl.BlockSpec((cmp_len, dk, hid), z3),
            pl.BlockSpec((2, hid), z2),
            pl.BlockSpec((hid, LANES), z2), pl.BlockSpec((hid, LANES), z2),
            pl.BlockSpec((1, LANES), z2),
            pl.BlockSpec((nblk, LANES), z2), pl.BlockSpec((nblk, LANES), z2)]


def compress_prompt(z, cmpw, *, nb, t, col_block, dk, hid, cmp_len):
    nblk = t // CMP_STRIDE
    return pl.pallas_call(
        functools.partial(_cmp_prompt_body, nblk=nblk, dk=dk, hid=hid),
        grid=(nb,),
        in_specs=[pl.BlockSpec((t, LANES), lambda b: (b, col_block))] + _cmp_weight_specs(cmp_len, dk, hid, nblk, 1),
        out_specs=pl.BlockSpec((None, nblk, LANES), lambda b: (b, 0, 0)),
        out_shape=jax.ShapeDtypeStruct((nb, nblk, LANES), F32),
        compiler_params=_cparams("parallel"),
        name="compress_prompt",
    )(z, *cmpw)


def _select_rank(imp, qpos, ns):
    lane = lax.broadcasted_iota(jnp.int32, imp.shape, 1)
    cur = qpos // SLC_BLOCK
    forced = (lane == 0) | (lane == cur) | (lane == cur - 1)
    valid = lane * SLC_BLOCK <= qpos
    sc = jnp.where(forced, FORCE_SCORE, jnp.where(valid, imp, -FORCE_SCORE))
    sc = jnp.where(lane < ns, sc, -2.0 * FORCE_SCORE)
    rank = jnp.zeros(imp.shape, F32)
    for s in range(ns):
        col = sc[:, s:s + 1]
        beats = (col > sc) | ((col == sc) & (lane > s))
        rank = rank + jnp.where(beats, 1.0, 0.0)
    return rank, lane


def _select_mask(imp, qpos, ns):
    rank, lane = _select_rank(imp, qpos, ns)
    return jnp.where((rank < float(min(N_SELECT, ns))) & (lane < ns), 1.0, 0.0)


def _select_indices(imp, qpos, ns):
    rank, lane = _select_rank(imp, qpos, ns)
    lanef = lane.astype(F32)
    out_lane = lax.broadcasted_iota(jnp.int32, (imp.shape[0], LANES), 1)
    out = jnp.zeros((imp.shape[0], LANES), F32)
    for j in range(N_SELECT):
        idx = jnp.sum(jnp.where((rank == float(j)) & (lane < ns), lanef, 0.0), axis=-1, keepdims=True)
        out = jnp.where(out_lane == j, idx, out)
    return out.astype(jnp.int32)


def _overlap_matrix(ncp, cmp_len, width=LANES):
    n = lax.broadcasted_iota(jnp.int32, (ncp, width), 0)
    s = lax.broadcasted_iota(jnp.int32, (ncp, width), 1)
    cstart = n * CMP_STRIDE
    cend = cstart + cmp_len - 1
    bstart = s * SLC_BLOCK
    return jnp.where((cstart < bstart + SLC_BLOCK) & (cend >= bstart), 1.0, 0.0).astype(BF16)


def _cmpatt_body(q_ref, kv_ref, o_ref, sel_ref, *, heads, tq, dk, nc, cmp_len, ns):
    i = pl.program_id(1)
    rows = heads * tq
    ncp = kv_ref.shape[0]
    q = q_ref[...].reshape(rows, dk)
    kv = kv_ref[...]
    kc = kv[:, :dk].astype(BF16)
    vc = kv[:, dk:2 * dk].astype(BF16)
    s = lax.dot_general(q, kc, _NT, preferred_element_type=F32).reshape(heads, tq, ncp)
    qpos = i * tq + lax.broadcasted_iota(jnp.int32, (tq, ncp), 0)
    n = lax.broadcasted_iota(jnp.int32, (tq, ncp), 1)
    mask = ((n * CMP_STRIDE + cmp_len - 1 <= qpos) & (n < nc))[None]
    s = jnp.where(mask, s, NEG)
    m = jnp.max(s, axis=-1, keepdims=True)
    e = jnp.where(mask, jnp.exp(s - m), 0.0)
    p = e / jnp.maximum(jnp.sum(e, axis=-1, keepdims=True), 1e-30)
    o = jnp.dot(p.reshape(rows, ncp).astype(BF16), vc, preferred_element_type=F32)
    o_ref[...] = o.reshape(heads, tq, dk).astype(o_ref.dtype)
    psum = jnp.sum(p, axis=0)
    hi = psum.astype(BF16)
    lo = (psum - hi.astype(F32)).astype(BF16)
    ov = _overlap_matrix(ncp, cmp_len)
    imp = jnp.dot(hi, ov, preferred_element_type=F32) + jnp.dot(lo, ov, preferred_element_type=F32)
    qp = i * tq + lax.broadcasted_iota(jnp.int32, (tq, 1), 0)
    sel_ref[...] = _select_mask(imp, qp, ns).astype(sel_ref.dtype)


def prompt_cmp_attention(nq, kcvc, *, nb, t, dk, cmp_len):
    heads, m, _ = nq.shape
    tq = ATT_TQ
    nqb = t // tq
    ncp = kcvc.shape[1]
    nc = (t - cmp_len) // CMP_STRIDE + 1
    ns = -(-t // SLC_BLOCK)
    assert ns <= LANES and ncp <= LANES
    qmap = lambda b, i: (0, b * nqb + i, 0)
    return pl.pallas_call(
        functools.partial(_cmpatt_body, heads=heads, tq=tq, dk=dk, nc=nc, cmp_len=cmp_len, ns=ns),
        grid=(nb, nqb),
        in_specs=[pl.BlockSpec((heads, tq, dk), qmap), pl.BlockSpec((None, ncp, LANES), lambda b, i: (b, 0, 0))],
        out_specs=[pl.BlockSpec((heads, tq, dk), qmap), pl.BlockSpec((tq, LANES), lambda b, i: (b * nqb + i, 0))],
        out_shape=[jax.ShapeDtypeStruct((heads, m, dk), BF16), jax.ShapeDtypeStruct((m, LANES), BF16)],
        compiler_params=_cparams("parallel", "parallel"),
        name="prompt_cmp_att",
    )(nq, kcvc)


def _gate_body(oc_ref, os_ref, ow_ref, g_ref, e_ref, o_ref, *, heads, dk):
    sig = jax.nn.sigmoid(g_ref[...])
    hi = sig.astype(BF16)
    lo = (sig - hi.astype(F32)).astype(BF16)
    gates = [jnp.dot(hi, e_ref[i], preferred_element_type=F32) + jnp.dot(lo, e_ref[i], preferred_element_type=F32)
             for i in range(3)]
    for h in range(heads):
        sl = slice(h * dk, (h + 1) * dk)
        o = (gates[0][:, sl] * oc_ref[h].astype(F32) + gates[1][:, sl] * os_ref[h].astype(F32)
             + gates[2][:, sl] * ow_ref[h].astype(F32))
        o_ref[:, sl] = o.astype(o_ref.dtype)


def nsa_gate_combine(o_c, o_s, o_w, z, expand, *, gate_col_block):
    heads, m, dk = o_c.shape
    tm = _pick_tile(m, 640, LANES)
    hmap = lambda i: (0, i, 0)
    return pl.pallas_call(
        functools.partial(_gate_body, heads=heads, dk=dk),
        grid=(m // tm,),
        in_specs=[pl.BlockSpec((heads, tm, dk), hmap)] * 3
        + [pl.BlockSpec((tm, LANES), lambda i: (i, gate_col_block)),
           pl.BlockSpec((3, LANES, heads * dk), lambda i: (0, 0, 0))],
        out_specs=pl.BlockSpec((tm, heads * dk), lambda i: (i, 0)),
        out_shape=jax.ShapeDtypeStruct((m, heads * dk), BF16),
        compiler_params=_cparams("parallel"),
        name="nsa_gate_combine",
    )(o_c, o_s, o_w, z, expand)


def _headmm_body(x_ref, w_ref, o_ref):
    o_ref[...] = jnp.dot(x_ref[...], w_ref[...].astype(BF16), preferred_element_type=F32).astype(o_ref.dtype)


def head_matmul(x, w, *, layer):
    heads, m, k = x.shape
    n = w.shape[3]
    tm = _pick_tile(m, 1664, LANES)
    return pl.pallas_call(
        _headmm_body,
        grid=(m // tm, heads),
        in_specs=[pl.BlockSpec((None, tm, k), lambda i, h: (h, i, 0)),
                  pl.BlockSpec((None, None, k, n), lambda i, h: (layer, h, 0, 0))],
        out_specs=pl.BlockSpec((tm, n), lambda i, h: (i, h)),
        out_shape=jax.ShapeDtypeStruct((m, heads * n), BF16),
        compiler_params=_cparams("parallel", "parallel"),
        name="head_matmul",
    )(x, w)


def _rms_body(x_ref, g_ref, o_ref):
    x = x_ref[...]
    o_ref[...] = (x * _row_rms(x) * g_ref[...]).astype(o_ref.dtype)


def rms_norm_rows(x, g):
    m, d = x.shape
    tm = _pick_tile(m, 640, LANES)
    return pl.pallas_call(
        _rms_body,
        grid=(m // tm,),
        in_specs=[pl.BlockSpec((tm, d), lambda i: (i, 0)), pl.BlockSpec((1, d), lambda i: (0, 0))],
        out_specs=pl.BlockSpec((tm, d), lambda i: (i, 0)),
        out_shape=jax.ShapeDtypeStruct((m, d), BF16),
        compiler_params=_cparams("parallel"),
        name="rms_norm_rows",
    )(x, g.reshape(1, d))


PAGES_PER_STEP = 16


def _page_specs(layer, tile, n_pages, gp):
    def mk(i):
        return pl.BlockSpec((None, None) + tile, lambda b, g, pt: (layer, pt[b * n_pages + g * gp + i], 0, 0))
    return [mk(i) for i in range(gp)]


def _cat_pages(refs, axis=0):
    return jnp.concatenate([r[...] for r in refs], axis=axis) if len(refs) > 1 else refs[0][...]


def _token_minor(pool):
    return jnp.swapaxes(pool, 2, 3)


def _online_update(m_scr, l_scr, acc_scr, s, v_bf, mask=None):
    if mask is not None:
        s = jnp.where(mask, s, NEG)
    m_prev = m_scr[...]
    m_new = jnp.maximum(m_prev, jnp.max(s, axis=-1, keepdims=True))
    alpha = jnp.exp(m_prev - m_new)
    p = jnp.exp(s - m_new)
    if mask is not None:
        p = jnp.where(mask, p, 0.0)
    l_scr[...] = alpha * l_scr[...] + jnp.sum(p, axis=-1, keepdims=True)
    acc_scr[...] = alpha * acc_scr[...] + jnp.dot(p.astype(BF16), v_bf, preferred_element_type=F32)
    m_scr[...] = m_new


def _smla_body(pt_ref, *refs, gp, heads, nope, rope):
    del pt_ref
    lat = refs[:gp]
    krp = refs[gp:2 * gp]
    qa_ref, qr_ref, cs_ref, krs_ref, wt_ref, o_ref, m_scr, l_scr, acc_scr = refs[2 * gp:]
    g = pl.program_id(1)

    @pl.when(g == 0)
    def _():
        m_scr[...] = jnp.full(m_scr.shape, NEG, F32)
        l_scr[...] = jnp.zeros(l_scr.shape, F32)
        acc_scr[...] = jnp.zeros(acc_scr.shape, F32)

    qa, qr, wt = qa_ref[...], qr_ref[...], wt_ref[...]

    def attend(c_bf, s_rope, mask):
        r = _key_scale(c_bf, wt, heads, nope)
        s = lax.dot_general(qa, c_bf, _NT, preferred_element_type=F32) * r + s_rope
        _online_update(m_scr, l_scr, acc_scr, s, c_bf, mask)

    kr_t = _cat_pages(krp, axis=1).astype(BF16)
    attend(_cat_pages(lat).astype(BF16), jnp.dot(qr, kr_t, preferred_element_type=F32), None)

    @pl.when(g == pl.num_programs(1) - 1)
    def _():
        c_new = jnp.broadcast_to(cs_ref[...], (LANES, cs_ref.shape[1])).astype(BF16)
        s_rope = jnp.sum(qr.astype(F32) * krs_ref[...][:, :rope], axis=-1, keepdims=True)
        attend(c_new, s_rope, lax.broadcasted_iota(jnp.int32, (heads, LANES), 1) == 0)
        o_ref[...] = (acc_scr[...] / l_scr[...]).astype(o_ref.dtype)


def sample_mla(page_table_flat, lat_pool, krope_pool_t, qabs_s, qrope_s, c_new, tb_new, wt_bf, *, layer, heads, nope):
    db, _, kv_rank = qabs_s.shape
    rope = qrope_s.shape[2]
    page_size = lat_pool.shape[2]
    n_pages = page_table_flat.shape[0] // db
    gp = min(PAGES_PER_STEP, n_pages)
    assert n_pages % gp == 0
    per_b = lambda w1, w2: pl.BlockSpec((None, w1, w2), lambda b, g, pt: (b, 0, 0))
    grid_spec = pltpu.PrefetchScalarGridSpec(
        num_scalar_prefetch=1,
        grid=(db, n_pages // gp),
        in_specs=_page_specs(layer, (page_size, kv_rank), n_pages, gp) + _page_specs(layer, (rope, page_size), n_pages, gp)
        + [per_b(heads, kv_rank), per_b(heads, rope), per_b(1, kv_rank), per_b(1, tb_new.shape[2]),
           pl.BlockSpec((heads * nope, kv_rank), lambda b, g, pt: (0, 0))],
        out_specs=per_b(heads, kv_rank),
        scratch_shapes=[pltpu.VMEM((heads, 1), F32), pltpu.VMEM((heads, 1), F32), pltpu.VMEM((heads, kv_rank), F32)],
    )
    return pl.pallas_call(
        functools.partial(_smla_body, gp=gp, heads=heads, nope=nope, rope=rope),
        grid_spec=grid_spec,
        out_shape=jax.ShapeDtypeStruct((db, heads, kv_rank), BF16),
        compiler_params=_cparams("parallel", "arbitrary"),
        name="sample_mla",
    )(page_table_flat, *([lat_pool] * gp), *([krope_pool_t] * gp), qabs_s, qrope_s, c_new, tb_new, wt_bf)


def _scmp_body(pt_ref, *refs, gp, heads, dk, hid, page_size, nblk, nc, cmp_len, ns, nsp, past_len):
    del pt_ref
    kpg = refs[:gp]
    vpg = refs[gp:2 * gp]
    (q_ref, pos_ref, w1k_ref, w1v_ref, b1_ref, w2k_ref, w2v_ref, gk_ref, cos_ref, sin_ref,
     oc_ref, sel_ref, rk_scr, rv_scr) = refs[2 * gp:]
    g = pl.program_id(1)
    for i in range(gp):
        row0 = pl.multiple_of((g * gp + i) * page_size, page_size)
        rk_scr[pl.ds(row0, page_size), :] = kpg[i][...].T
        rv_scr[pl.ds(row0, page_size), :] = vpg[i][...].T

    @pl.when(g == pl.num_programs(1) - 1)
    def _():
        def load_kv(j):
            return (rk_scr[pl.ds(j, nblk, stride=CMP_STRIDE), :], rv_scr[pl.ds(j, nblk, stride=CMP_STRIDE), :])
        kv = _compress_core(load_kv, nblk, pos_ref, w1k_ref, w1v_ref, b1_ref, w2k_ref, w2v_ref, gk_ref,
                            cos_ref, sin_ref, dk=dk, hid=hid)
        kc = kv[:, :dk].astype(BF16)
        vc = kv[:, dk:2 * dk].astype(BF16)
        s = lax.dot_general(q_ref[...], kc, _NT, preferred_element_type=F32)
        n = lax.broadcasted_iota(jnp.int32, s.shape, 1)
        mask = (n * CMP_STRIDE + cmp_len - 1 <= past_len) & (n < nc)
        s = jnp.where(mask, s, NEG)
        mx = jnp.max(s, axis=-1, keepdims=True)
        e = jnp.where(mask, jnp.exp(s - mx), 0.0)
        p = e / jnp.maximum(jnp.sum(e, axis=-1, keepdims=True), 1e-30)
        oc_ref[...] = jnp.dot(p.astype(BF16), vc, preferred_element_type=F32)
        psum = jnp.broadcast_to(jnp.sum(p, axis=0, keepdims=True), (8, nblk))
        hi = psum.astype(BF16)
        lo = (psum - hi.astype(F32)).astype(BF16)
        ov = _overlap_matrix(nblk, cmp_len, nsp)
        imp = jnp.dot(hi, ov, preferred_element_type=F32) + jnp.dot(lo, ov, preferred_element_type=F32)
        sel = _select_indices(imp, jnp.full((8, 1), past_len, jnp.int32), ns)
        sel_ref[...] = sel[0:1, :]


def sample_compress_attend(page_table_flat, kpool, vpool, nq_s, cmpw, *, layer, cmp_len, hid, past_len):
    db, heads, dk = nq_s.shape
    page_size = kpool.shape[3]
    n_pages = page_table_flat.shape[0] // db
    gp = min(PAGES_PER_STEP, n_pages)
    nblk = past_len // CMP_STRIDE
    nc = (past_len + 1 - cmp_len) // CMP_STRIDE + 1
    ns = -(-(past_len + 1) // SLC_BLOCK)
    nsp = -(-ns // LANES) * LANES
    assert n_pages % gp == 0 and nc <= nblk and ns >= N_SELECT
    z2 = lambda b, g, pt: (0, 0)
    z3 = lambda b, g, pt: (0, 0, 0)
    wspecs = [pl.BlockSpec((cmp_len, 2 * dk), z2), pl.BlockSpec((cmp_len, dk, hid), z3),
              pl.BlockSpec((cmp_len, dk, hid), z3), pl.BlockSpec((2, hid), z2), pl.BlockSpec((hid, LANES), z2),
              pl.BlockSpec((hid, LANES), z2), pl.BlockSpec((1, LANES), z2),
              pl.BlockSpec((nblk, LANES), z2), pl.BlockSpec((nblk, LANES), z2)]
    grid_spec = pltpu.PrefetchScalarGridSpec(
        num_scalar_prefetch=1,
        grid=(db, n_pages // gp),
        in_specs=_page_specs(layer, (dk, page_size), n_pages, gp) + _page_specs(layer, (dk, page_size), n_pages, gp)
        + [pl.BlockSpec((None, heads, dk), lambda b, g, pt: (b, 0, 0))] + wspecs,
        out_specs=[pl.BlockSpec((None, heads, dk), lambda b, g, pt: (b, 0, 0)),
                   pl.BlockSpec((None, 1, LANES), lambda b, g, pt: (b, 0, 0))],
        scratch_shapes=[pltpu.VMEM((past_len, dk), F32), pltpu.VMEM((past_len, dk), F32)],
    )
    return pl.pallas_call(
        functools.partial(_scmp_body, gp=gp, heads=heads, dk=dk, hid=hid, page_size=page_size, nblk=nblk, nc=nc,
                          cmp_len=cmp_len, ns=ns, nsp=nsp, past_len=past_len),
        grid_spec=grid_spec,
        out_shape=[jax.ShapeDtypeStruct((db, heads, dk), F32), jax.ShapeDtypeStruct((db, 1, LANES), jnp.int32)],
        compiler_params=_cparams("parallel", "arbitrary"),
        name="sample_compress_attend",
    )(page_table_flat, *([kpool] * gp), *([vpool] * gp), nq_s, *cmpw)


def _sel_page(si, b, j, bpp, n_pages):
    return jnp.clip(si[b * LANES + j] // bpp, 0, n_pages - 1)


def _ssel_body(pt_ref, si_ref, *refs, nsel, heads, dk, page_size, n_pages, w0):
    del pt_ref
    kpg = refs[:nsel]
    vpg = refs[nsel:2 * nsel]
    q_ref, ta_ref, td_ref, kwin_ref, vwin_ref, oc_ref, ng_ref, o_ref = refs[2 * nsel:]
    b = pl.program_id(0)
    bpp = page_size // SLC_BLOCK
    q = q_ref[...]
    qf = q.astype(F32)
    ta = ta_ref[...]
    td = td_ref[...]

    lane = lax.broadcasted_iota(jnp.int32, (1, page_size), 1)
    pieces = []
    for j in range(nsel):
        blk = si_ref[b * LANES + j]
        blk = jnp.where(blk < n_pages * bpp, blk, -1)
        tok_blk = (_sel_page(si_ref, b, j, bpp, n_pages) * page_size + lane) // SLC_BLOCK
        pieces.append(jnp.where(tok_blk == blk, 1.0, 0.0))
    mask = jnp.concatenate(pieces, axis=1) > 0.5
    s = jnp.dot(q, _cat_pages(kpg, axis=1).astype(BF16), preferred_element_type=F32)
    mask = jnp.broadcast_to(mask, s.shape)
    s = jnp.where(mask, s, NEG)
    s_new = jnp.sum(qf * ta[:, :dk], axis=-1, keepdims=True)
    ms = jnp.maximum(jnp.max(s, axis=-1, keepdims=True), s_new)
    p = jnp.where(mask, jnp.exp(s - ms), 0.0)
    p_new = jnp.exp(s_new - ms)
    pv = lax.dot_general(p.astype(BF16), _cat_pages(vpg, axis=1).astype(BF16), _NT, preferred_element_type=F32)
    o_s = (pv + p_new * td[:, :dk]) / (jnp.sum(p, axis=-1, keepdims=True) + p_new)

    sw = jnp.dot(q, kwin_ref[...].astype(BF16), preferred_element_type=F32)
    wmask = lax.broadcasted_iota(jnp.int32, sw.shape, 1) >= w0
    sw = jnp.where(wmask, sw, NEG)
    sw_new = jnp.sum(qf * ta[:, dk:2 * dk], axis=-1, keepdims=True)
    mw = jnp.maximum(jnp.max(sw, axis=-1, keepdims=True), sw_new)
    pw = jnp.where(wmask, jnp.exp(sw - mw), 0.0)
    pw_new = jnp.exp(sw_new - mw)
    o_w = ((lax.dot_general(pw.astype(BF16), vwin_ref[...].astype(BF16), _NT, preferred_element_type=F32)
            + pw_new * td[:, dk:2 * dk]) / (jnp.sum(pw, axis=-1, keepdims=True) + pw_new))
    gate = jax.nn.sigmoid(ng_ref[...])
    o = gate[:, 0:1] * oc_ref[...] + gate[:, 1:2] * o_s + gate[:, 2:3] * o_w
    o_ref[...] = o.astype(o_ref.dtype)


def sample_select_window(page_table_flat, sel_flat, kpool, vpool, kwin, vwin, nq_s, ta_new, td_new, o_c, ngate_s, *,
                         layer):
    db, heads, dk = nq_s.shape
    page_size = kpool.shape[3]
    n_pages = page_table_flat.shape[0] // db
    bpp = page_size // SLC_BLOCK
    wbuf = kwin.shape[3]
    w0 = max(0, wbuf + 1 - WINDOW)
    per_b = lambda w1, w2: pl.BlockSpec((None, w1, w2), lambda b, pt, si: (b, 0, 0))
    win = pl.BlockSpec((None, None, dk, wbuf), lambda b, pt, si: (layer, b, 0, 0))

    def sel_spec(j):
        return pl.BlockSpec((None, None, dk, page_size),
                            lambda b, pt, si: (layer, pt[b * n_pages + _sel_page(si, b, j, bpp, n_pages)], 0, 0))
    sel_specs = [sel_spec(j) for j in range(N_SELECT)]
    grid_spec = pltpu.PrefetchScalarGridSpec(
        num_scalar_prefetch=2,
        grid=(db,),
        in_specs=sel_specs + sel_specs
        + [per_b(heads, dk), per_b(1, LANES), per_b(1, LANES), win, win, per_b(heads, dk), per_b(heads, 3)],
        out_specs=per_b(heads, dk),
    )
    return pl.pallas_call(
        functools.partial(_ssel_body, nsel=N_SELECT, heads=heads, dk=dk, page_size=page_size, n_pages=n_pages, w0=w0),
        grid_spec=grid_spec,
        out_shape=jax.ShapeDtypeStruct((db, heads, dk), BF16),
        compiler_params=_cparams("parallel"),
        name="sample_select_window",
    )(page_table_flat, sel_flat, *([kpool] * N_SELECT), *([vpool] * N_SELECT), nq_s, ta_new, td_new, kwin, vwin, o_c,
      ngate_s)


def _rope_tables(pos, group, rot_dim):
    half = rot_dim // 2
    inv = ROPE_THETA ** (-jnp.arange(half, dtype=F32) / half)
    ang = pos.astype(F32)[:, None] * inv[None, :]
    cos, sin = jnp.cos(ang), jnp.sin(ang)
    ones = jnp.ones((pos.shape[0], group - rot_dim), F32)
    c = jnp.concatenate([cos, cos, ones], axis=1)
    s = jnp.concatenate([-sin, sin, 0.0 * ones], axis=1)
    reps = LANES // group
    return jnp.tile(c, (1, reps)), jnp.tile(s, (1, reps))


def kernel(x_prompt, x_sample, cache_mla_latent, cache_mla_krope, cache_nsa_kcmp, cache_nsa_vcmp,
           cache_nsa_kslc, cache_nsa_vslc, state_nsa_kwin, state_nsa_vwin, page_table,
           g_attn, w_in, g_qa, w_qb, g_kva, g_qn, g_qr, g_kn, g_kr, w_kb, w_vb, w_mla_o,
           g_nq, g_nkc, g_nks, g_nkw, cmp_pos, cmp_w1, cmp_b1, cmp_w2, w_nsa_o,
           w_out, g_mlp, w_up, w_down):
    nb, t_p, d_model = x_prompt.shape
    db, t_s, _ = x_sample.shape
    assert t_s == 1
    depth = w_in.shape[0]
    q_rank = g_qa.shape[1]
    kv_rank = g_kva.shape[1]
    nope = g_qn.shape[1]
    rope = g_qr.shape[1]
    heads = w_kb.shape[1]
    mla_v = w_vb.shape[3]
    dk = g_nq.shape[1]
    nsa_heads = w_nsa_o.shape[1] // dk
    cmp_len = cmp_pos.shape[2]
    hid = cmp_w1.shape[3]
    page_size = cache_mla_latent.shape[2]
    past_len = page_table.shape[1] * page_size
    wkeep_p = min(WINDOW, t_p)
    mla_scale = (nope + rope) ** -0.5
    assert cmp_len == 2 * CMP_STRIDE and 2 * dk == LANES and 2 * rope == LANES and nope == LANES

    n_p = nb * t_p
    n_s = db * t_s
    m = n_p + n_s
    nqw = nsa_heads * dk

    o_nq = q_rank + kv_rank
    o_ta = o_nq + nqw
    o_tb = o_ta + LANES
    o_tc = o_tb + LANES
    o_td = o_tc + LANES
    o_te = o_td + LANES
    tn_g = min(MM_TN, d_model)
    o_ga = -(-(o_te + LANES) // tn_g) * tn_g
    o_gb = o_ga + d_model
    assert 3 * nsa_heads <= LANES and o_gb % tn_g == 0
    widths = [q_rank, kv_rank, rope, nqw] + [dk] * 6 + [3 * nsa_heads, d_model, d_model]
    off = np.cumsum([0] + widths)
    seg = lambda w, i: w[:, int(off[i]):int(off[i + 1])]

    pos_rows = jnp.concatenate([jnp.tile(jnp.arange(t_p, dtype=jnp.int32), nb), jnp.full((n_s,), past_len, jnp.int32)])
    cn, sn = _rope_tables(pos_rows, dk, dk // 4)
    cm, sm = _rope_tables(pos_rows, rope, rope)
    nblk_p = t_p // CMP_STRIDE
    cend_p = jnp.arange(nblk_p, dtype=jnp.int32) * CMP_STRIDE + cmp_len - 1
    cc_p, sc_p = _rope_tables(cend_p, dk, dk // 4)
    lane = jnp.arange(LANES)
    cc_p = jnp.where(lane[None, :] < dk, cc_p, 1.0)
    sc_p = jnp.where(lane[None, :] < dk, sc_p, 0.0)
    cend_s = jnp.arange(past_len // CMP_STRIDE, dtype=jnp.int32) * CMP_STRIDE + cmp_len - 1
    cc_s, sc_s = _rope_tables(cend_s, dk, dk // 4)
    cc_s = jnp.where(lane[None, :] < dk, cc_s, 1.0)
    sc_s = jnp.where(lane[None, :] < dk, sc_s, 0.0)
    pt_flat = page_table.reshape(-1)
    krope_t, kcmp_t, vcmp_t, kslc_t, vslc_t, kwin_t, vwin_t = [
        _token_minor(a) for a in (cache_mla_krope, cache_nsa_kcmp, cache_nsa_vcmp, cache_nsa_kslc, cache_nsa_vslc,
                                  state_nsa_kwin, state_nsa_vwin)]

    gi = jnp.arange(LANES)[:, None]
    hd = jnp.arange(nqw)[None, :] // dk
    expand = jnp.stack([jnp.where(gi == 3 * hd + i, 1.0, 0.0) for i in range(3)]).astype(BF16)

    x = jnp.concatenate([x_prompt.reshape(n_p, d_model), x_sample.reshape(n_s, d_model)], axis=0)
    p_rows, s_rows = [], []
    tk_mla = _pick_tile(t_p, 512, LANES)

    for l in range(depth):
        wl = w_in[l]
        zc = jnp.zeros((d_model, dk), F32)
        w_perm = jnp.concatenate(
            [seg(wl, 0), seg(wl, 1), seg(wl, 3), seg(wl, 6), seg(wl, 8), seg(wl, 2), zc, seg(wl, 4), seg(wl, 5),
             seg(wl, 7), seg(wl, 9), seg(wl, 10), jnp.zeros((d_model, o_ga - o_te - 3 * nsa_heads), F32),
             seg(wl, 11), seg(wl, 12)], axis=1)
        wq = w_qb[l].reshape(q_rank, heads, nope + rope)
        wq_perm = jnp.concatenate([wq[..., :nope].reshape(q_rank, heads * nope),
                                   wq[..., nope:].reshape(q_rank, heads * rope)], axis=1)
        wkb_t = jnp.transpose(w_kb[l], (0, 2, 1)).reshape(heads * nope, kv_rank).astype(BF16)

        h = rms_norm_rows(x, g_attn[l])
        z = matmul(h, w_perm)
        g_a = jnp.concatenate([g_nks[l], g_nkw[l]]).reshape(1, LANES)
        g_b = jnp.concatenate([g_kr[l], jnp.ones((LANES - rope,), F32)]).reshape(1, LANES)
        qa, c, nq, ta, tb = post_in_proj(
            z, g_qa[l].reshape(1, -1), g_kva[l].reshape(1, -1), jnp.tile(g_nq[l], nsa_heads).reshape(1, -1),
            g_a, g_b, cn, sn, cm, sm, q_rank=q_rank, kv_rank=kv_rank, nsa_heads=nsa_heads, dk=dk, mla_rope=rope)
        q = matmul(qa, wq_perm, tn=1024)
        q_abs, q_rope = post_q_proj(
            q, (g_qn[l] * g_kn[l]).reshape(1, -1), jnp.tile(g_qr[l], LANES // rope).reshape(1, -1), w_kb, cm, sm,
            layer=l, heads=heads, nope=nope, rope=rope, kv_rank=kv_rank, scale=mla_scale)

        r_p = prompt_key_scale(c, wkb_t, nb=nb, t=t_p, heads=heads, nope=nope, tk=tk_mla)
        o_lat = prompt_attention(q_abs, c, c, mode="causal", nb=nb, t=t_p, tk=tk_mla, d1=kv_rank, dv=kv_rank,
                                 q2=q_rope, k2=tb, d2=rope, r=r_p)
        cmpw = (jnp.concatenate([cmp_pos[l, 0], cmp_pos[l, 1]], axis=1),
                cmp_w1[l, 0].reshape(cmp_len, dk, hid), cmp_w1[l, 1].reshape(cmp_len, dk, hid), cmp_b1[l],
                jnp.pad(cmp_w2[l, 0], ((0, 0), (0, LANES - dk))), jnp.pad(cmp_w2[l, 1], ((0, 0), (LANES - dk, 0))),
                jnp.concatenate([g_nkc[l], jnp.ones((LANES - dk,), F32)]).reshape(1, LANES))
        kcvc_p = compress_prompt(z, cmpw + (cc_p, sc_p), nb=nb, t=t_p, col_block=o_tc // LANES, dk=dk, hid=hid,
                                 cmp_len=cmp_len)
        o_c, sel = prompt_cmp_attention(nq, kcvc_p, nb=nb, t=t_p, dk=dk, cmp_len=cmp_len)
        o_s = prompt_attention(nq, ta, z, mode="select", nb=nb, t=t_p, tk=tk_mla, d1=dk, dv=dk, sel=sel,
                               v_col_block=o_td // LANES)
        o_w = prompt_window_attention(nq, ta, z, nb=nb, t=t_p, d=dk, k_off=dk, v_off=dk, v_col_block=o_td // LANES)

        sl_s = slice(n_p, m)
        c_s = c[sl_s][:, None, :]
        kr_s = tb[sl_s, :rope][:, None, :]
        kcr_s = z[sl_s, o_tc:o_tc + dk][:, None, :]
        vcr_s = z[sl_s, o_tc + dk:o_tc + 2 * dk][:, None, :]
        ks_s = ta[sl_s, :dk][:, None, :]
        kw_s = ta[sl_s, dk:][:, None, :]
        vs_s = z[sl_s, o_td:o_td + dk][:, None, :]
        vw_s = z[sl_s, o_td + dk:o_td + 2 * dk][:, None, :]
        ng_s = z[sl_s, o_te:o_te + 3 * nsa_heads].reshape(n_s, nsa_heads, 3)
        qabs_s = jnp.transpose(q_abs[:, sl_s], (1, 0, 2))
        qrope_s = jnp.transpose(q_rope[:, sl_s], (1, 0, 2))
        nq_s = jnp.transpose(nq[:, sl_s], (1, 0, 2))
        olat_s = sample_mla(pt_flat, cache_mla_latent, krope_t, qabs_s, qrope_s, c_s, tb[sl_s][:, None, :],
                            wkb_t, layer=l, heads=heads, nope=nope)
        oc_s, sel_s = sample_compress_attend(pt_flat, kcmp_t, vcmp_t, nq_s, cmpw + (cc_s, sc_s),
                                             layer=l, cmp_len=cmp_len, hid=hid, past_len=past_len)
        onsa_s = sample_select_window(pt_flat, sel_s.reshape(-1), kslc_t, vslc_t, kwin_t, vwin_t, nq_s,
                                      ta[sl_s][:, None, :], z[sl_s, o_td:o_td + LANES][:, None, :], oc_s, ng_s,
                                      layer=l)
        kw_all = jnp.concatenate([state_nsa_kwin[l], kw_s], axis=1)
        vw_all = jnp.concatenate([state_nsa_vwin[l], vw_s], axis=1)

        o_lat = lax.dynamic_update_slice(o_lat, jnp.transpose(olat_s, (1, 0, 2)), (0, n_p, 0))
        o_mla = head_matmul(o_lat, w_vb, layer=l)
        o_nsa = nsa_gate_combine(o_c, o_s, o_w, z, expand, gate_col_block=o_te // LANES)
        o_nsa = lax.dynamic_update_slice(o_nsa, onsa_s.reshape(n_s, nqw), (n_p, 0))
        y1 = matmul(o_mla, w_mla_o, layer=l, epilogue="gate", extras=(z,), extra_col_blocks=(o_ga // tn_g,), tn=tn_g)
        y = matmul(o_nsa, w_nsa_o, layer=l, epilogue="gate_add", extras=(z, y1), extra_col_blocks=(o_gb // tn_g, 0),
                   out_dtype=BF16, tn=tn_g)
        x = matmul(y, w_out, layer=l, epilogue="add", extras=(x,))
        h2 = rms_norm_rows(x, g_mlp[l])
        u = matmul(h2, w_up, layer=l, epilogue="relu2", out_dtype=BF16)
        x = matmul(u, w_down, layer=l, epilogue="add", extras=(x,))

        pr = lambda a: a[:n_p].reshape(nb, t_p, -1)
        kw_p = pr(ta[:, dk:])
        vw_p = pr(z[:, o_td + dk:o_td + 2 * dk])
        p_rows.append((pr(c), pr(tb[:, :rope]), pr(z[:, o_tc:o_tc + dk]), pr(z[:, o_tc + dk:o_tc + 2 * dk]),
                       pr(ta[:, :dk]), pr(z[:, o_td:o_td + dk]), kw_p[:, t_p - wkeep_p:], vw_p[:, t_p - wkeep_p:]))
        s_rows.append((c_s, kr_s, kcr_s, vcr_s, ks_s, vs_s, kw_all[:, t_s:], vw_all[:, t_s:]))

    xp = x[:n_p].reshape(nb, t_p, d_model)
    xs = x[n_p:].reshape(db, t_s, d_model)
    (lat_p, krope_p, kcmp_p, vcmp_p, kslc_p, vslc_p, kwin_p, vwin_p) = [jnp.stack(a, axis=0) for a in zip(*p_rows)]
    (lat_s, krope_s, kcmp_s, vcmp_s, kslc_s, vslc_s, kwin_s, vwin_s) = [jnp.stack(a, axis=0) for a in zip(*s_rows)]
    return (xp, xs, lat_p, lat_s, krope_p, krope_s, kcmp_p, kcmp_s, vcmp_p, vcmp_s,
            kslc_p, kslc_s, vslc_p, vslc_s, kwin_p, kwin_s, vwin_p, vwin_s)
```

```python
import functools

import jax
import jax.numpy as jnp
import numpy as np
from jax import lax
from jax.experimental import pallas as pl
from jax.experimental.pallas import tpu as pltpu

CMP_STRIDE = 16
SLC_BLOCK = 64
N_SELECT = 16
WINDOW = 512
ROPE_THETA = 500000.0
EPS = 1e-6
NEG = -1e30
FORCE_SCORE = 1e9

F32 = jnp.float32
BF16 = jnp.bfloat16
LANES = 128
ATT_TQ = 128
KEY_SCALE_CHUNK = 256

VMEM_LIMIT_BYTES = 56 * 1024 * 1024

_NT = (((1,), (1,)), ((), ()))


def _cparams(*sem):
    return pltpu.CompilerParams(dimension_semantics=sem, vmem_limit_bytes=VMEM_LIMIT_BYTES)


def _pick_tile(dim, target, quantum):
    best = None
    t = quantum
    while t <= min(dim, target):
        if dim % t == 0:
            best = t
        t += quantum
    return best if best is not None else dim


def _mm_body(*refs, nk, epilogue, n_extra):
    a_ref, b_ref = refs[0], refs[1]
    extras = refs[2:2 + n_extra]
    o_ref = refs[2 + n_extra]
    acc_ref = refs[3 + n_extra] if nk > 1 else None

    part = jnp.dot(a_ref[...].astype(BF16), b_ref[...].astype(BF16), preferred_element_type=F32)

    def finish(acc):
        if epilogue == "none":
            r = acc
        elif epilogue == "relu2":
            r = jnp.square(jnp.maximum(acc, 0.0))
        elif epilogue == "add":
            r = acc + extras[0][...]
        elif epilogue == "gate":
            r = jax.nn.sigmoid(extras[0][...]) * acc
        elif epilogue == "gate_add":
            r = jax.nn.sigmoid(extras[0][...]) * acc + extras[1][...]
        else:
            raise ValueError(epilogue)
        o_ref[...] = r.astype(o_ref.dtype)

    if nk == 1:
        finish(part)
        return

    k = pl.program_id(2)

    @pl.when(k == 0)
    def _():
        acc_ref[...] = part

    @pl.when(k > 0)
    def _():
        acc_ref[...] += part

    @pl.when(k == nk - 1)
    def _():
        finish(acc_ref[...])


MM_TM, MM_TN, MM_TK = 1664, 256, 4096


def matmul(a, b, *, layer=None, epilogue="none", extras=(), extra_col_blocks=None, out_dtype=F32, tm=MM_TM, tn=MM_TN,
           tk=MM_TK):
    m, kdim = a.shape
    kdim2, n = b.shape[-2:]
    assert kdim == kdim2 and (b.ndim == 2) == (layer is None)
    tm = _pick_tile(m, tm, LANES)
    tk = _pick_tile(kdim, tk, LANES)
    tn = min(tn, n)
    nk = kdim // tk
    grid = (m // tm, pl.cdiv(n, tn), nk)
    if layer is None:
        b_spec = pl.BlockSpec((tk, tn), lambda i, j, k: (k, j))
    else:
        b_spec = pl.BlockSpec((None, tk, tn), lambda i, j, k: (layer, k, j))
    in_specs = [pl.BlockSpec((tm, tk), lambda i, j, k: (i, k)), b_spec]
    offs = extra_col_blocks or (0,) * len(extras)
    for o in offs:
        in_specs.append(pl.BlockSpec((tm, tn), lambda i, j, k, o=o: (i, j + o)))
    scratch = [pltpu.VMEM((tm, tn), F32)] if nk > 1 else []
    return pl.pallas_call(
        functools.partial(_mm_body, nk=nk, epilogue=epilogue, n_extra=len(extras)),
        grid=grid,
        in_specs=in_specs,
        out_specs=pl.BlockSpec((tm, tn), lambda i, j, k: (i, j)),
        out_shape=jax.ShapeDtypeStruct((m, n), out_dtype),
        scratch_shapes=scratch,
        compiler_params=_cparams("parallel", "parallel", "arbitrary"),
        name="mm_" + epilogue,
    )(a, b, *extras)


def _group_mean_sq(x, group):
    w = x.shape[1]
    xx = x * x
    hi = xx.astype(BF16)
    lo = (xx - hi.astype(F32)).astype(BF16)
    r = lax.broadcasted_iota(jnp.int32, (LANES, LANES), 0) // group
    c = lax.broadcasted_iota(jnp.int32, (LANES, LANES), 1) // group
    bd = jnp.where(r == c, 1.0, 0.0).astype(BF16)
    outs = []
    for j in range(w // LANES):
        sl = slice(j * LANES, (j + 1) * LANES)
        outs.append(jnp.dot(hi[:, sl], bd, preferred_element_type=F32)
                    + jnp.dot(lo[:, sl], bd, preferred_element_type=F32))
    ms = outs[0] if len(outs) == 1 else jnp.concatenate(outs, axis=1)
    return ms * (1.0 / group)


def _rope128(y, cos, sin, half, group):
    lane = lax.broadcasted_iota(jnp.int32, y.shape, 1) % group
    partner = jnp.where(lane < half, pltpu.roll(y, LANES - half, 1), pltpu.roll(y, half, 1))
    return y * cos + partner * sin


def _row_rms(x):
    return lax.rsqrt(jnp.mean(x * x, axis=-1, keepdims=True) + EPS)


def _p1_body(z_ref, gqa_ref, gkva_ref, gnq_ref, ga_ref, gb_ref, cn_ref, sn_ref, cm_ref, sm_ref,
             qa_ref, c_ref, nq_ref, ta_ref, tb_ref, *, q_rank, kv_rank, nqw, dk, nsa_half, mla_half, nsa_scale):
    o1 = q_rank + kv_rank
    x = z_ref[:, 0:q_rank]
    qa_ref[...] = (x * _row_rms(x) * gqa_ref[...]).astype(qa_ref.dtype)
    x = z_ref[:, q_rank:o1]
    c_ref[...] = x * _row_rms(x) * gkva_ref[...]
    cn, sn = cn_ref[...], sn_ref[...]
    heads_per_tile = LANES // dk
    for j in range(nqw // LANES):
        x = z_ref[:, o1 + j * LANES:o1 + (j + 1) * LANES]
        y = x * lax.rsqrt(_group_mean_sq(x, dk) + EPS) * gnq_ref[:, j * LANES:(j + 1) * LANES]
        y = _rope128(y, cn, sn, nsa_half, dk) * nsa_scale
        for hh in range(heads_per_tile):
            nq_ref[j * heads_per_tile + hh] = y[:, hh * dk:(hh + 1) * dk].astype(nq_ref.dtype)
    o2 = o1 + nqw
    x = z_ref[:, o2:o2 + LANES]
    y = x * lax.rsqrt(_group_mean_sq(x, dk) + EPS) * ga_ref[...]
    ta_ref[...] = _rope128(y, cn, sn, nsa_half, dk)
    x = z_ref[:, o2 + LANES:o2 + 2 * LANES]
    y = x * lax.rsqrt(_group_mean_sq(x, 2 * mla_half) + EPS) * gb_ref[...]
    tb_ref[...] = _rope128(y, cm_ref[...], sm_ref[...], mla_half, 2 * mla_half)


def post_in_proj(z, gqa, gkva, gnq_t, g_a, g_b, cn, sn, cm, sm, *, q_rank, kv_rank, nsa_heads, dk, mla_rope):
    m = z.shape[0]
    nqw = nsa_heads * dk
    win = q_rank + kv_rank + nqw + 2 * LANES
    tm = _pick_tile(m, 640, LANES)
    row = lambda i: (i, 0)
    cst = lambda i: (0, 0)
    return pl.pallas_call(
        functools.partial(_p1_body, q_rank=q_rank, kv_rank=kv_rank, nqw=nqw, dk=dk, nsa_half=dk // 8,
                          mla_half=mla_rope // 2, nsa_scale=dk ** -0.5),
        grid=(m // tm,),
        in_specs=[pl.BlockSpec((tm, win), row),
                  pl.BlockSpec((1, q_rank), cst), pl.BlockSpec((1, kv_rank), cst), pl.BlockSpec((1, nqw), cst),
                  pl.BlockSpec((1, LANES), cst), pl.BlockSpec((1, LANES), cst),
                  pl.BlockSpec((tm, LANES), row), pl.BlockSpec((tm, LANES), row),
                  pl.BlockSpec((tm, LANES), row), pl.BlockSpec((tm, LANES), row)],
        out_specs=[pl.BlockSpec((tm, q_rank), row), pl.BlockSpec((tm, kv_rank), row),
                   pl.BlockSpec((nsa_heads, tm, dk), lambda i: (0, i, 0)),
                   pl.BlockSpec((tm, LANES), row), pl.BlockSpec((tm, LANES), row)],
        out_shape=[jax.ShapeDtypeStruct((m, q_rank), BF16), jax.ShapeDtypeStruct((m, kv_rank), F32),
                   jax.ShapeDtypeStruct((nsa_heads, m, dk), BF16),
                   jax.ShapeDtypeStruct((m, LANES), F32), jax.ShapeDtypeStruct((m, LANES), F32)],
        compiler_params=_cparams("parallel"),
        name="post_in_proj",
    )(z, gqa, gkva, gnq_t, g_a, g_b, cn, sn, cm, sm)


def _p2_body(q_ref, gn_ref, gr_ref, wkb_ref, cm_ref, sm_ref, qabs_ref, qr_ref, *, heads, nope, rope, scale):
    for h in range(heads):
        x = q_ref[:, h * nope:(h + 1) * nope]
        y = x * lax.rsqrt(_group_mean_sq(x, nope) + EPS) * gn_ref[...] * scale
        qa = lax.dot_general(y.astype(BF16), wkb_ref[h].astype(BF16), _NT, preferred_element_type=F32)
        qabs_ref[h] = qa.astype(qabs_ref.dtype)
    o = heads * nope
    per_tile = LANES // rope
    cm, sm = cm_ref[...], sm_ref[...]
    for j in range(heads * rope // LANES):
        x = q_ref[:, o + j * LANES:o + (j + 1) * LANES]
        y = x * lax.rsqrt(_group_mean_sq(x, rope) + EPS) * gr_ref[...]
        y = _rope128(y, cm, sm, rope // 2, rope) * scale
        for hh in range(per_tile):
            qr_ref[j * per_tile + hh] = y[:, hh * rope:(hh + 1) * rope].astype(qr_ref.dtype)


def post_q_proj(q, gn, gr, w_kb, cm, sm, *, layer, heads, nope, rope, kv_rank, scale):
    m = q.shape[0]
    tm = _pick_tile(m, 640, LANES)
    row = lambda i: (i, 0)
    cst = lambda i: (0, 0)
    return pl.pallas_call(
        functools.partial(_p2_body, heads=heads, nope=nope, rope=rope, scale=scale),
        grid=(m // tm,),
        in_specs=[pl.BlockSpec((tm, heads * (nope + rope)), row),
                  pl.BlockSpec((1, nope), cst), pl.BlockSpec((1, LANES), cst),
                  pl.BlockSpec((None, heads, kv_rank, nope), lambda i: (layer, 0, 0, 0)),
                  pl.BlockSpec((tm, LANES), row), pl.BlockSpec((tm, LANES), row)],
        out_specs=[pl.BlockSpec((heads, tm, kv_rank), lambda i: (0, i, 0)),
                   pl.BlockSpec((heads, tm, rope), lambda i: (0, i, 0))],
        out_shape=[jax.ShapeDtypeStruct((heads, m, kv_rank), BF16), jax.ShapeDtypeStruct((heads, m, rope), BF16)],
        compiler_params=_cparams("parallel"),
        name="post_q_proj",
    )(q, gn, gr, w_kb, cm, sm)


def _key_scale(c_bf, wt_bf, heads, nope):
    n = c_bf.shape[0]
    step = min(n, KEY_SCALE_CHUNK)
    parts = []
    for t0 in range(0, n, step):
        kt = lax.dot_general(wt_bf, c_bf[t0:t0 + step], _NT, preferred_element_type=F32)
        kt = kt * kt
        parts.append(jnp.sum(kt.reshape(heads, nope, step), axis=1))
    ss = parts[0] if len(parts) == 1 else jnp.concatenate(parts, axis=1)
    return lax.rsqrt(ss * (1.0 / nope) + EPS)


def _ks_body(c_ref, wt_ref, r_ref, *, heads, nope):
    r_ref[...] = _key_scale(c_ref[...].astype(BF16), wt_ref[...], heads, nope)


def prompt_key_scale(c, wt_bf, *, nb, t, heads, nope, tk):
    kv_rank = c.shape[1]
    nch = t // tk
    return pl.pallas_call(
        functools.partial(_ks_body, heads=heads, nope=nope),
        grid=(nb, nch),
        in_specs=[pl.BlockSpec((tk, kv_rank), lambda b, j: (b * nch + j, 0)),
                  pl.BlockSpec((heads * nope, kv_rank), lambda b, j: (0, 0))],
        out_specs=pl.BlockSpec((None, None, heads, tk), lambda b, j: (b, j, 0, 0)),
        out_shape=jax.ShapeDtypeStruct((nb, nch, heads, tk), F32),
        compiler_params=_cparams("parallel", "parallel"),
        name="prompt_key_scale",
    )(c, wt_bf)


def _att_body(*refs, mode, heads, tq, tk, d1, d2, dv, k1_off, k2_off, v_off, has_r):
    it = iter(refs)
    q1_ref = next(it)
    q2_ref = next(it) if d2 else None
    k1_ref = next(it)
    k2_ref = next(it) if d2 else None
    v_ref = next(it)
    r_ref = next(it) if has_r else None
    sel_ref = next(it) if mode == "select" else None
    o_ref, m_scr, l_scr, acc_scr = next(it), next(it), next(it), next(it)

    i = pl.program_id(1)
    q0 = i * tq
    rows = heads * tq
    q1 = q1_ref[...].reshape(rows, d1)
    q2 = q2_ref[...].reshape(rows, d2) if d2 else None

    m_scr[...] = jnp.full(m_scr.shape, NEG, F32)
    l_scr[...] = jnp.zeros(l_scr.shape, F32)
    acc_scr[...] = jnp.zeros(acc_scr.shape, F32)

    hi = (q0 + tq + tk - 1) // tk

    def chunk(kci, carry):
        ks = pl.multiple_of(kci * tk, tk)
        k1c = k1_ref[pl.ds(ks, tk), k1_off:k1_off + d1].astype(BF16)
        s = lax.dot_general(q1, k1c, _NT, preferred_element_type=F32).reshape(heads, tq, tk)
        if has_r:
            s = s * r_ref[kci][:, None, :]
        if d2:
            k2c = k2_ref[pl.ds(ks, tk), k2_off:k2_off + d2].astype(BF16)
            s = s + lax.dot_general(q2, k2c, _NT, preferred_element_type=F32).reshape(heads, tq, tk)
        qpos = q0 + lax.broadcasted_iota(jnp.int32, (tq, tk), 0)
        kpos = ks + lax.broadcasted_iota(jnp.int32, (tq, tk), 1)
        dist = qpos - kpos
        if mode == "select":
            blk = (ks + lax.broadcasted_iota(jnp.int32, (LANES, tk), 1)) // SLC_BLOCK
            e = jnp.where(blk == lax.broadcasted_iota(jnp.int32, (LANES, tk), 0), 1.0, 0.0).astype(BF16)
            selm = jnp.dot(sel_ref[...], e, preferred_element_type=F32)
            mask = jnp.where(dist >= 0, selm, 0.0) > 0.5
        else:
            mask = dist >= 0
        mask = mask[None]
        s = jnp.where(mask, s, NEG)
        m_prev = m_scr[...]
        m_new = jnp.maximum(m_prev, jnp.max(s, axis=-1, keepdims=True))
        alpha = jnp.exp(m_prev - m_new)
        p = jnp.where(mask, jnp.exp(s - m_new), 0.0)
        l_scr[...] = alpha * l_scr[...] + jnp.sum(p, axis=-1, keepdims=True)
        vc = v_ref[pl.ds(ks, tk), v_off:v_off + dv].astype(BF16)
        pv = jnp.dot(p.reshape(rows, tk).astype(BF16), vc, preferred_element_type=F32)
        acc_scr[...] = alpha * acc_scr[...] + pv.reshape(heads, tq, dv)
        m_scr[...] = m_new
        return carry

    lax.fori_loop(0, hi, chunk, 0)
    o_ref[...] = (acc_scr[...] / l_scr[...]).astype(o_ref.dtype)


def prompt_attention(q1, k1, v, *, mode, nb, t, tk, d1, dv, k1_off=0, v_off=0, q2=None, k2=None, d2=0, k2_off=0,
                     r=None, sel=None, v_col_block=0):
    heads = q1.shape[0]
    tq = ATT_TQ
    nq = t // tq
    qmap = lambda b, i: (0, b * nq + i, 0)
    kmap = lambda b, i: (b, 0)
    args, specs = [q1], [pl.BlockSpec((heads, tq, d1), qmap)]
    if d2:
        args.append(q2)
        specs.append(pl.BlockSpec((heads, tq, d2), qmap))
    args.append(k1)
    specs.append(pl.BlockSpec((t, k1.shape[1]), kmap))
    if d2:
        args.append(k2)
        specs.append(pl.BlockSpec((t, k2.shape[1]), kmap))
    args.append(v)
    vw = LANES if v.shape[1] > max(LANES, dv) else v.shape[1]
    specs.append(pl.BlockSpec((t, vw), lambda b, i: (b, v_col_block)))
    if r is not None:
        args.append(r)
        specs.append(pl.BlockSpec((None, t // tk, heads, tk), lambda b, i: (b, 0, 0, 0)))
    if sel is not None:
        args.append(sel)
        specs.append(pl.BlockSpec((tq, LANES), lambda b, i: (b * nq + i, 0)))
    return pl.pallas_call(
        functools.partial(_att_body, mode=mode, heads=heads, tq=tq, tk=tk, d1=d1, d2=d2, dv=dv,
                          k1_off=k1_off, k2_off=k2_off, v_off=v_off, has_r=r is not None),
        grid=(nb, nq),
        in_specs=specs,
        out_specs=pl.BlockSpec((heads, tq, dv), qmap),
        out_shape=jax.ShapeDtypeStruct((heads, q1.shape[1], dv), BF16),
        scratch_shapes=[pltpu.VMEM((heads, tq, 1), F32), pltpu.VMEM((heads, tq, 1), F32),
                        pltpu.VMEM((heads, tq, dv), F32)],
        compiler_params=_cparams("parallel", "parallel"),
        name="prompt_att_" + mode,
    )(*args)


def _win_body(q_ref, k_ref, v_ref, o_ref, *, heads, tq, wk, d, k_off, v_off):
    q0 = pl.program_id(1) * tq
    rows = heads * tq
    q = q_ref[...].reshape(rows, d)
    start = pl.multiple_of(jnp.maximum(q0 + tq - wk, 0), tq)
    kc = k_ref[pl.ds(start, wk), k_off:k_off + d].astype(BF16)
    s = lax.dot_general(q, kc, _NT, preferred_element_type=F32).reshape(heads, tq, wk)
    dist = (q0 - start) + lax.broadcasted_iota(jnp.int32, (tq, wk), 0) - lax.broadcasted_iota(jnp.int32, (tq, wk), 1)
    mask = ((dist >= 0) & (dist < WINDOW))[None]
    s = jnp.where(mask, s, NEG)
    p = jnp.where(mask, jnp.exp(s - jnp.max(s, axis=-1, keepdims=True)), 0.0)
    l = jnp.sum(p, axis=-1, keepdims=True)
    vc = v_ref[pl.ds(start, wk), v_off:v_off + d].astype(BF16)
    o = jnp.dot(p.reshape(rows, wk).astype(BF16), vc, preferred_element_type=F32).reshape(heads, tq, d)
    o_ref[...] = (o / l).astype(o_ref.dtype)


def prompt_window_attention(q, k, v, *, nb, t, d, k_off, v_off, v_col_block):
    heads = q.shape[0]
    tq = ATT_TQ
    nq = t // tq
    wk = min(WINDOW + tq, t)
    assert wk % tq == 0
    qmap = lambda b, i: (0, b * nq + i, 0)
    return pl.pallas_call(
        functools.partial(_win_body, heads=heads, tq=tq, wk=wk, d=d, k_off=k_off, v_off=v_off),
        grid=(nb, nq),
        in_specs=[pl.BlockSpec((heads, tq, d), qmap), pl.BlockSpec((t, k.shape[1]), lambda b, i: (b, 0)),
                  pl.BlockSpec((t, LANES), lambda b, i: (b, v_col_block))],
        out_specs=pl.BlockSpec((heads, tq, d), qmap),
        out_shape=jax.ShapeDtypeStruct((heads, q.shape[1], d), BF16),
        compiler_params=_cparams("parallel", "parallel"),
        name="prompt_att_window",
    )(q, k, v)


def _compress_core(rows_ref, nblk, ptop_ref, pbot_ref, wtop_ref, wbot_ref, b1_ref, w2_ref, g_ref, cos_ref, sin_ref,
                   *, dk):
    x = jnp.concatenate([rows_ref[pl.ds(j, nblk, stride=CMP_STRIDE), :] for j in range(CMP_STRIDE)], axis=1)
    top = jnp.dot((x + ptop_ref[...]).astype(BF16), wtop_ref[...], preferred_element_type=F32)
    bot = jnp.dot((x + pbot_ref[...]).astype(BF16), wbot_ref[...], preferred_element_type=F32)
    h = top + pltpu.roll(bot, nblk - 1, 0) + b1_ref[...]
    kv = jnp.dot(jax.nn.gelu(h).astype(BF16), w2_ref[...], preferred_element_type=F32)
    lane = lax.broadcasted_iota(jnp.int32, kv.shape, 1)
    y = jnp.where(lane < dk, kv * lax.rsqrt(_group_mean_sq(kv, dk) + EPS) * g_ref[...], kv)
    return _rope128(y, cos_ref[...], sin_ref[...], dk // 8, dk)


def pack_compress_weights(cmp_pos_l, cmp_w1_l, cmp_b1_l, cmp_w2_l, g_nkc_l):
    _, cmp_len, dk = cmp_pos_l.shape
    hid = cmp_w1_l.shape[2]
    half = cmp_len // 2
    pos = jnp.concatenate([cmp_pos_l[0], cmp_pos_l[1]], axis=1)
    w1k = cmp_w1_l[0].reshape(cmp_len, dk, hid)
    w1v = cmp_w1_l[1].reshape(cmp_len, dk, hid)
    zeros = jnp.zeros_like(w1k)
    w1 = jnp.concatenate([jnp.concatenate([w1k, zeros], axis=2), jnp.concatenate([zeros, w1v], axis=2)], axis=1)
    zk = jnp.zeros((hid, dk), F32)
    w2 = jnp.concatenate([jnp.concatenate([cmp_w2_l[0], zk], axis=1), jnp.concatenate([zk, cmp_w2_l[1]], axis=1)], axis=0)
    return (pos[:half].reshape(1, -1), pos[half:].reshape(1, -1),
            w1[:half].reshape(half * 2 * dk, 2 * hid).astype(BF16), w1[half:].reshape(half * 2 * dk, 2 * hid).astype(BF16),
            cmp_b1_l.reshape(1, 2 * hid), w2.astype(BF16),
            jnp.concatenate([g_nkc_l, jnp.ones((LANES - dk,), F32)]).reshape(1, LANES))


def _cmp_prompt_body(rows_ref, ptop_ref, pbot_ref, wtop_ref, wbot_ref, b1_ref, w2_ref, g_ref, cos_ref, sin_ref, o_ref,
                     *, nblk, dk):
    o_ref[...] = _compress_core(rows_ref, nblk, ptop_ref, pbot_ref, wtop_ref, wbot_ref, b1_ref, w2_ref, g_ref,
                                cos_ref, sin_ref, dk=dk)


def _cmp_weight_specs(dk, hid, nblk):
    z2 = (lambda *a: (0, 0))
    wide = CMP_STRIDE * 2 * dk
    return [pl.BlockSpec((1, wide), z2), pl.BlockSpec((1, wide), z2),
            pl.BlockSpec((wide, 2 * hid), z2), pl.BlockSpec((wide, 2 * hid), z2),
            pl.BlockSpec((1, 2 * hid), z2), pl.BlockSpec((2 * hid, LANES), z2), pl.BlockSpec((1, LANES), z2),
            pl.BlockSpec((nblk, LANES), z2), pl.BlockSpec((nblk, LANES), z2)]


def compress_prompt(z, cmpw, *, nb, t, col_block, dk, hid):
    nblk = t // CMP_STRIDE
    return pl.pallas_call(
        functools.partial(_cmp_prompt_body, nblk=nblk, dk=dk),
        grid=(nb,),
        in_specs=[pl.BlockSpec((t, LANES), lambda b: (b, col_block))] + _cmp_weight_specs(dk, hid, nblk),
        out_specs=pl.BlockSpec((None, nblk, LANES), lambda b: (b, 0, 0)),
        out_shape=jax.ShapeDtypeStruct((nb, nblk, LANES), F32),
        compiler_params=_cparams("parallel"),
        name="compress_prompt",
    )(z, *cmpw)


def _select_rank(imp, qpos, ns):
    lane = lax.broadcasted_iota(jnp.int32, imp.shape, 1)
    cur = qpos // SLC_BLOCK
    forced = (lane == 0) | (lane == cur) | (lane == cur - 1)
    valid = lane * SLC_BLOCK <= qpos
    sc = jnp.where(forced, FORCE_SCORE, jnp.where(valid, imp, -FORCE_SCORE))
    sc = jnp.where(lane < ns, sc, -2.0 * FORCE_SCORE)
    rank = jnp.zeros(imp.shape, F32)
    for s in range(ns):
        col = sc[:, s:s + 1]
        beats = (col > sc) | ((col == sc) & (lane > s))
        rank = rank + jnp.where(beats, 1.0, 0.0)
    return rank, lane


def _select_mask(imp, qpos, ns):
    rank, lane = _select_rank(imp, qpos, ns)
    return jnp.where((rank < float(min(N_SELECT, ns))) & (lane < ns), 1.0, 0.0)


def _select_indices(imp, qpos, ns):
    rank, lane = _select_rank(imp, qpos, ns)
    lanef = lane.astype(F32)
    out_lane = lax.broadcasted_iota(jnp.int32, (imp.shape[0], LANES), 1)
    out = jnp.zeros((imp.shape[0], LANES), F32)
    for j in range(N_SELECT):
        idx = jnp.sum(jnp.where((rank == float(j)) & (lane < ns), lanef, 0.0), axis=-1, keepdims=True)
        out = jnp.where(out_lane == j, idx, out)
    return out.astype(jnp.int32)


def _overlap_matrix(ncp, cmp_len, width=LANES):
    n = lax.broadcasted_iota(jnp.int32, (ncp, width), 0)
    s = lax.broadcasted_iota(jnp.int32, (ncp, width), 1)
    cstart = n * CMP_STRIDE
    cend = cstart + cmp_len - 1
    bstart = s * SLC_BLOCK
    return jnp.where((cstart < bstart + SLC_BLOCK) & (cend >= bstart), 1.0, 0.0).astype(BF16)


def _cmpatt_body(q_ref, kv_ref, o_ref, sel_ref, *, heads, tq, dk, nc, cmp_len, ns):
    i = pl.program_id(1)
    rows = heads * tq
    ncp = kv_ref.shape[0]
    q = q_ref[...].reshape(rows, dk)
    kv = kv_ref[...]
    kc = kv[:, :dk].astype(BF16)
    vc = kv[:, dk:2 * dk].astype(BF16)
    s = lax.dot_general(q, kc, _NT, preferred_element_type=F32).reshape(heads, tq, ncp)
    qpos = i * tq + lax.broadcasted_iota(jnp.int32, (tq, ncp), 0)
    n = lax.broadcasted_iota(jnp.int32, (tq, ncp), 1)
    mask = ((n * CMP_STRIDE + cmp_len - 1 <= qpos) & (n < nc))[None]
    s = jnp.where(mask, s, NEG)
    m = jnp.max(s, axis=-1, keepdims=True)
    e = jnp.where(mask, jnp.exp(s - m), 0.0)
    p = e / jnp.maximum(jnp.sum(e, axis=-1, keepdims=True), 1e-30)
    o = jnp.dot(p.reshape(rows, ncp).astype(BF16), vc, preferred_element_type=F32)
    o_ref[...] = o.reshape(heads, tq, dk).astype(o_ref.dtype)
    psum = jnp.sum(p, axis=0)
    hi = psum.astype(BF16)
    lo = (psum - hi.astype(F32)).astype(BF16)
    ov = _overlap_matrix(ncp, cmp_len)
    imp = jnp.dot(hi, ov, preferred_element_type=F32) + jnp.dot(lo, ov, preferred_element_type=F32)
    qp = i * tq + lax.broadcasted_iota(jnp.int32, (tq, 1), 0)
    sel_ref[...] = _select_mask(imp, qp, ns).astype(sel_ref.dtype)


def prompt_cmp_attention(nq, kcvc, *, nb, t, dk, cmp_len):
    heads, m, _ = nq.shape
    tq = ATT_TQ
    nqb = t // tq
    ncp = kcvc.shape[1]
    nc = (t - cmp_len) // CMP_STRIDE + 1
    ns = -(-t // SLC_BLOCK)
    assert ns <= LANES and ncp <= LANES
    qmap = lambda b, i: (0, b * nqb + i, 0)
    return pl.pallas_call(
        functools.partial(_cmpatt_body, heads=heads, tq=tq, dk=dk, nc=nc, cmp_len=cmp_len, ns=ns),
        grid=(nb, nqb),
        in_specs=[pl.BlockSpec((heads, tq, dk), qmap), pl.BlockSpec((None, ncp, LANES), lambda b, i: (b, 0, 0))],
        out_specs=[pl.BlockSpec((heads, tq, dk), qmap), pl.BlockSpec((tq, LANES), lambda b, i: (b * nqb + i, 0))],
        out_shape=[jax.ShapeDtypeStruct((heads, m, dk), BF16), jax.ShapeDtypeStruct((m, LANES), BF16)],
        compiler_params=_cparams("parallel", "parallel"),
        name="prompt_cmp_att",
    )(nq, kcvc)


def _gate_body(oc_ref, os_ref, ow_ref, g_ref, e_ref, o_ref, *, heads, dk):
    sig = jax.nn.sigmoid(g_ref[...])
    hi = sig.astype(BF16)
    lo = (sig - hi.astype(F32)).astype(BF16)
    gates = [jnp.dot(hi, e_ref[i], preferred_element_type=F32) + jnp.dot(lo, e_ref[i], preferred_element_type=F32)
             for i in range(3)]
    for h in range(heads):
        sl = slice(h * dk, (h + 1) * dk)
        o = (gates[0][:, sl] * oc_ref[h].astype(F32) + gates[1][:, sl] * os_ref[h].astype(F32)
             + gates[2][:, sl] * ow_ref[h].astype(F32))
        o_ref[:, sl] = o.astype(o_ref.dtype)


def nsa_gate_combine(o_c, o_s, o_w, z, expand, *, gate_col_block):
    heads, m, dk = o_c.shape
    tm = _pick_tile(m, 640, LANES)
    hmap = lambda i: (0, i, 0)
    return pl.pallas_call(
        functools.partial(_gate_body, heads=heads, dk=dk),
        grid=(m // tm,),
        in_specs=[pl.BlockSpec((heads, tm, dk), hmap)] * 3
        + [pl.BlockSpec((tm, LANES), lambda i: (i, gate_col_block)),
           pl.BlockSpec((3, LANES, heads * dk), lambda i: (0, 0, 0))],
        out_specs=pl.BlockSpec((tm, heads * dk), lambda i: (i, 0)),
        out_shape=jax.ShapeDtypeStruct((m, heads * dk), BF16),
        compiler_params=_cparams("parallel"),
        name="nsa_gate_combine",
    )(o_c, o_s, o_w, z, expand)


def _headmm_body(x_ref, w_ref, o_ref):
    o_ref[...] = jnp.dot(x_ref[...], w_ref[...].astype(BF16), preferred_element_type=F32).astype(o_ref.dtype)


def head_matmul(x, w, *, layer):
    heads, m, k = x.shape
    n = w.shape[3]
    tm = _pick_tile(m, 1664, LANES)
    return pl.pallas_call(
        _headmm_body,
        grid=(m // tm, heads),
        in_specs=[pl.BlockSpec((None, tm, k), lambda i, h: (h, i, 0)),
                  pl.BlockSpec((None, None, k, n), lambda i, h: (layer, h, 0, 0))],
        out_specs=pl.BlockSpec((tm, n), lambda i, h: (i, h)),
        out_shape=jax.ShapeDtypeStruct((m, heads * n), BF16),
        compiler_params=_cparams("parallel", "parallel"),
        name="head_matmul",
    )(x, w)


def _rms_body(x_ref, g_ref, o_ref):
    x = x_ref[...]
    o_ref[...] = (x * _row_rms(x) * g_ref[...]).astype(o_ref.dtype)


def rms_norm_rows(x, g):
    m, d = x.shape
    tm = _pick_tile(m, 640, LANES)
    return pl.pallas_call(
        _rms_body,
        grid=(m // tm,),
        in_specs=[pl.BlockSpec((tm, d), lambda i: (i, 0)), pl.BlockSpec((1, d), lambda i: (0, 0))],
        out_specs=pl.BlockSpec((tm, d), lambda i: (i, 0)),
        out_shape=jax.ShapeDtypeStruct((m, d), BF16),
        compiler_params=_cparams("parallel"),
        name="rms_norm_rows",
    )(x, g.reshape(1, d))


PAGES_PER_STEP = 16


def _page_specs(layer, tile, n_pages, gp):
    def mk(i):
        return pl.BlockSpec((None, None) + tile, lambda b, g, pt: (layer, pt[b * n_pages + g * gp + i], 0, 0))
    return [mk(i) for i in range(gp)]


def _cat_pages(refs, axis=0):
    return jnp.concatenate([r[...] for r in refs], axis=axis) if len(refs) > 1 else refs[0][...]


def _token_minor(pool):
    return jnp.swapaxes(pool, 2, 3)


def _online_update(m_scr, l_scr, acc_scr, s, v_bf, mask=None):
    if mask is not None:
        s = jnp.where(mask, s, NEG)
    m_prev = m_scr[...]
    m_new = jnp.maximum(m_prev, jnp.max(s, axis=-1, keepdims=True))
    alpha = jnp.exp(m_prev - m_new)
    p = jnp.exp(s - m_new)
    if mask is not None:
        p = jnp.where(mask, p, 0.0)
    l_scr[...] = alpha * l_scr[...] + jnp.sum(p, axis=-1, keepdims=True)
    acc_scr[...] = alpha * acc_scr[...] + jnp.dot(p.astype(BF16), v_bf, preferred_element_type=F32)
    m_scr[...] = m_new


def _smla_body(pt_ref, *refs, gp, heads, nope, rope):
    del pt_ref
    lat = refs[:gp]
    krp = refs[gp:2 * gp]
    qa_ref, qr_ref, cs_ref, krs_ref, wt_ref, o_ref, m_scr, l_scr, acc_scr = refs[2 * gp:]
    g = pl.program_id(1)

    @pl.when(g == 0)
    def _():
        m_scr[...] = jnp.full(m_scr.shape, NEG, F32)
        l_scr[...] = jnp.zeros(l_scr.shape, F32)
        acc_scr[...] = jnp.zeros(acc_scr.shape, F32)

    qa, qr, wt = qa_ref[...], qr_ref[...], wt_ref[...]

    def attend(c_bf, s_rope, mask):
        r = _key_scale(c_bf, wt, heads, nope)
        s = lax.dot_general(qa, c_bf, _NT, preferred_element_type=F32) * r + s_rope
        _online_update(m_scr, l_scr, acc_scr, s, c_bf, mask)

    kr_t = _cat_pages(krp, axis=1).astype(BF16)
    attend(_cat_pages(lat).astype(BF16), jnp.dot(qr, kr_t, preferred_element_type=F32), None)

    @pl.when(g == pl.num_programs(1) - 1)
    def _():
        c_new = jnp.broadcast_to(cs_ref[...], (LANES, cs_ref.shape[1])).astype(BF16)
        s_rope = jnp.sum(qr.astype(F32) * krs_ref[...][:, :rope], axis=-1, keepdims=True)
        attend(c_new, s_rope, lax.broadcasted_iota(jnp.int32, (heads, LANES), 1) == 0)
        o_ref[...] = (acc_scr[...] / l_scr[...]).astype(o_ref.dtype)


def sample_mla(page_table_flat, lat_pool, krope_pool_t, qabs_s, qrope_s, c_new, tb_new, wt_bf, *, layer, heads, nope):
    db, _, kv_rank = qabs_s.shape
    rope = qrope_s.shape[2]
    page_size = lat_pool.shape[2]
    n_pages = page_table_flat.shape[0] // db
    gp = min(PAGES_PER_STEP, n_pages)
    assert n_pages % gp == 0
    per_b = lambda w1, w2: pl.BlockSpec((None, w1, w2), lambda b, g, pt: (b, 0, 0))
    grid_spec = pltpu.PrefetchScalarGridSpec(
        num_scalar_prefetch=1,
        grid=(db, n_pages // gp),
        in_specs=_page_specs(layer, (page_size, kv_rank), n_pages, gp) + _page_specs(layer, (rope, page_size), n_pages, gp)
        + [per_b(heads, kv_rank), per_b(heads, rope), per_b(1, kv_rank), per_b(1, tb_new.shape[2]),
           pl.BlockSpec((heads * nope, kv_rank), lambda b, g, pt: (0, 0))],
        out_specs=per_b(heads, kv_rank),
        scratch_shapes=[pltpu.VMEM((heads, 1), F32), pltpu.VMEM((heads, 1), F32), pltpu.VMEM((heads, kv_rank), F32)],
    )
    return pl.pallas_call(
        functools.partial(_smla_body, gp=gp, heads=heads, nope=nope, rope=rope),
        grid_spec=grid_spec,
        out_shape=jax.ShapeDtypeStruct((db, heads, kv_rank), BF16),
        compiler_params=_cparams("parallel", "arbitrary"),
        name="sample_mla",
    )(page_table_flat, *([lat_pool] * gp), *([krope_pool_t] * gp), qabs_s, qrope_s, c_new, tb_new, wt_bf)


def _scmp_body(pt_ref, *refs, gp, heads, dk, page_size, nblk, nc, cmp_len, ns, nsp, past_len):
    del pt_ref
    kpg = refs[:gp]
    vpg = refs[gp:2 * gp]
    (q_ref, ptop_ref, pbot_ref, wtop_ref, wbot_ref, b1_ref, w2_ref, gk_ref, cos_ref, sin_ref,
     oc_ref, sel_ref, rows_scr) = refs[2 * gp:]
    g = pl.program_id(1)
    for i in range(gp):
        row0 = pl.multiple_of((g * gp + i) * page_size, page_size)
        rows_scr[pl.ds(row0, page_size), :] = jnp.concatenate([kpg[i][...], vpg[i][...]], axis=0).T

    @pl.when(g == pl.num_programs(1) - 1)
    def _():
        kv = _compress_core(rows_scr, nblk, ptop_ref, pbot_ref, wtop_ref, wbot_ref, b1_ref, w2_ref, gk_ref,
                            cos_ref, sin_ref, dk=dk)
        kc = kv[:, :dk].astype(BF16)
        vc = kv[:, dk:2 * dk].astype(BF16)
        s = lax.dot_general(q_ref[...], kc, _NT, preferred_element_type=F32)
        n = lax.broadcasted_iota(jnp.int32, s.shape, 1)
        mask = (n * CMP_STRIDE + cmp_len - 1 <= past_len) & (n < nc)
        s = jnp.where(mask, s, NEG)
        mx = jnp.max(s, axis=-1, keepdims=True)
        e = jnp.where(mask, jnp.exp(s - mx), 0.0)
        p = e / jnp.maximum(jnp.sum(e, axis=-1, keepdims=True), 1e-30)
        oc_ref[...] = jnp.dot(p.astype(BF16), vc, preferred_element_type=F32)
        psum = jnp.broadcast_to(jnp.sum(p, axis=0, keepdims=True), (8, nblk))
        hi = psum.astype(BF16)
        lo = (psum - hi.astype(F32)).astype(BF16)
        ov = _overlap_matrix(nblk, cmp_len, nsp)
        imp = jnp.dot(hi, ov, preferred_element_type=F32) + jnp.dot(lo, ov, preferred_element_type=F32)
        sel = _select_indices(imp, jnp.full((8, 1), past_len, jnp.int32), ns)
        sel_ref[...] = sel[0:1, :]


def sample_compress_attend(page_table_flat, kpool, vpool, nq_s, cmpw, *, layer, cmp_len, hid, past_len):
    db, heads, dk = nq_s.shape
    page_size = kpool.shape[3]
    n_pages = page_table_flat.shape[0] // db
    gp = min(PAGES_PER_STEP, n_pages)
    nblk = past_len // CMP_STRIDE
    nc = (past_len + 1 - cmp_len) // CMP_STRIDE + 1
    ns = -(-(past_len + 1) // SLC_BLOCK)
    nsp = -(-ns // LANES) * LANES
    assert n_pages % gp == 0 and nc <= nblk and ns >= N_SELECT and 2 * dk == LANES
    grid_spec = pltpu.PrefetchScalarGridSpec(
        num_scalar_prefetch=1,
        grid=(db, n_pages // gp),
        in_specs=_page_specs(layer, (dk, page_size), n_pages, gp) + _page_specs(layer, (dk, page_size), n_pages, gp)
        + [pl.BlockSpec((None, heads, dk), lambda b, g, pt: (b, 0, 0))] + _cmp_weight_specs(dk, hid, nblk),
        out_specs=[pl.BlockSpec((None, heads, dk), lambda b, g, pt: (b, 0, 0)),
                   pl.BlockSpec((None, 1, LANES), lambda b, g, pt: (b, 0, 0))],
        scratch_shapes=[pltpu.VMEM((past_len, LANES), F32)],
    )
    return pl.pallas_call(
        functools.partial(_scmp_body, gp=gp, heads=heads, dk=dk, page_size=page_size, nblk=nblk, nc=nc,
                          cmp_len=cmp_len, ns=ns, nsp=nsp, past_len=past_len),
        grid_spec=grid_spec,
        out_shape=[jax.ShapeDtypeStruct((db, heads, dk), F32), jax.ShapeDtypeStruct((db, 1, LANES), jnp.int32)],
        compiler_params=_cparams("parallel", "arbitrary"),
        name="sample_compress_attend",
    )(page_table_flat, *([kpool] * gp), *([vpool] * gp), nq_s, *cmpw)


def _sel_page(si, b, j, bpp, n_pages):
    return jnp.clip(si[b * LANES + j] // bpp, 0, n_pages - 1)


def _ssel_body(pt_ref, si_ref, *refs, nsel, heads, dk, page_size, n_pages, w0):
    del pt_ref
    kpg = refs[:nsel]
    vpg = refs[nsel:2 * nsel]
    q_ref, ta_ref, td_ref, kwin_ref, vwin_ref, oc_ref, ng_ref, o_ref = refs[2 * nsel:]
    b = pl.program_id(0)
    bpp = page_size // SLC_BLOCK
    q = q_ref[...]
    qf = q.astype(F32)
    ta = ta_ref[...]
    td = td_ref[...]

    lane = lax.broadcasted_iota(jnp.int32, (1, page_size), 1)
    pieces = []
    for j in range(nsel):
        blk = si_ref[b * LANES + j]
        blk = jnp.where(blk < n_pages * bpp, blk, -1)
        tok_blk = (_sel_page(si_ref, b, j, bpp, n_pages) * page_size + lane) // SLC_BLOCK
        pieces.append(jnp.where(tok_blk == blk, 1.0, 0.0))
    mask = jnp.concatenate(pieces, axis=1) > 0.5
    s = jnp.dot(q, _cat_pages(kpg, axis=1).astype(BF16), preferred_element_type=F32)
    mask = jnp.broadcast_to(mask, s.shape)
    s = jnp.where(mask, s, NEG)
    s_new = jnp.sum(qf * ta[:, :dk], axis=-1, keepdims=True)
    ms = jnp.maximum(jnp.max(s, axis=-1, keepdims=True), s_new)
    p = jnp.where(mask, jnp.exp(s - ms), 0.0)
    p_new = jnp.exp(s_new - ms)
    pv = lax.dot_general(p.astype(BF16), _cat_pages(vpg, axis=1).astype(BF16), _NT, preferred_element_type=F32)
    o_s = (pv + p_new * td[:, :dk]) / (jnp.sum(p, axis=-1, keepdims=True) + p_new)

    sw = jnp.dot(q, kwin_ref[...].astype(BF16), preferred_element_type=F32)
    wmask = lax.broadcasted_iota(jnp.int32, sw.shape, 1) >= w0
    sw = jnp.where(wmask, sw, NEG)
    sw_new = jnp.sum(qf * ta[:, dk:2 * dk], axis=-1, keepdims=True)
    mw = jnp.maximum(jnp.max(sw, axis=-1, keepdims=True), sw_new)
    pw = jnp.where(wmask, jnp.exp(sw - mw), 0.0)
    pw_new = jnp.exp(sw_new - mw)
    o_w = ((lax.dot_general(pw.astype(BF16), vwin_ref[...].astype(BF16), _NT, preferred_element_type=F32)
            + pw_new * td[:, dk:2 * dk]) / (jnp.sum(pw, axis=-1, keepdims=True) + pw_new))
    gate = jax.nn.sigmoid(ng_ref[...])
    o = gate[:, 0:1] * oc_ref[...] + gate[:, 1:2] * o_s + gate[:, 2:3] * o_w
    o_ref[...] = o.astype(o_ref.dtype)


def sample_select_window(page_table_flat, sel_flat, kpool, vpool, kwin, vwin, nq_s, ta_new, td_new, o_c, ngate_s, *,
                         layer):
    db, heads, dk = nq_s.shape
    page_size = kpool.shape[3]
    n_pages = page_table_flat.shape[0] // db
    bpp = page_size // SLC_BLOCK
    wbuf = kwin.shape[3]
    w0 = max(0, wbuf + 1 - WINDOW)
    per_b = lambda w1, w2: pl.BlockSpec((None, w1, w2), lambda b, pt, si: (b, 0, 0))
    win = pl.BlockSpec((None, None, dk, wbuf), lambda b, pt, si: (layer, b, 0, 0))

    def sel_spec(j):
        return pl.BlockSpec((None, None, dk, page_size),
                            lambda b, pt, si: (layer, pt[b * n_pages + _sel_page(si, b, j, bpp, n_pages)], 0, 0))
    sel_specs = [sel_spec(j) for j in range(N_SELECT)]
    grid_spec = pltpu.PrefetchScalarGridSpec(
        num_scalar_prefetch=2,
        grid=(db,),
        in_specs=sel_specs + sel_specs
        + [per_b(heads, dk), per_b(1, LANES), per_b(1, LANES), win, win, per_b(heads, dk), per_b(heads, 3)],
        out_specs=per_b(heads, dk),
    )
    return pl.pallas_call(
        functools.partial(_ssel_body, nsel=N_SELECT, heads=heads, dk=dk, page_size=page_size, n_pages=n_pages, w0=w0),
        grid_spec=grid_spec,
        out_shape=jax.ShapeDtypeStruct((db, heads, dk), BF16),
        compiler_params=_cparams("parallel"),
        name="sample_select_window",
    )(page_table_flat, sel_flat, *([kpool] * N_SELECT), *([vpool] * N_SELECT), nq_s, ta_new, td_new, kwin, vwin, o_c,
      ngate_s)


def _rope_tables(pos, group, rot_dim):
    half = rot_dim // 2
    inv = ROPE_THETA ** (-jnp.arange(half, dtype=F32) / half)
    ang = pos.astype(F32)[:, None] * inv[None, :]
    cos, sin = jnp.cos(ang), jnp.sin(ang)
    ones = jnp.ones((pos.shape[0], group - rot_dim), F32)
    c = jnp.concatenate([cos, cos, ones], axis=1)
    s = jnp.concatenate([-sin, sin, 0.0 * ones], axis=1)
    reps = LANES // group
    return jnp.tile(c, (1, reps)), jnp.tile(s, (1, reps))


def kernel(x_prompt, x_sample, cache_mla_latent, cache_mla_krope, cache_nsa_kcmp, cache_nsa_vcmp,
           cache_nsa_kslc, cache_nsa_vslc, state_nsa_kwin, state_nsa_vwin, page_table,
           g_attn, w_in, g_qa, w_qb, g_kva, g_qn, g_qr, g_kn, g_kr, w_kb, w_vb, w_mla_o,
           g_nq, g_nkc, g_nks, g_nkw, cmp_pos, cmp_w1, cmp_b1, cmp_w2, w_nsa_o,
           w_out, g_mlp, w_up, w_down):
    nb, t_p, d_model = x_prompt.shape
    db, t_s, _ = x_sample.shape
    assert t_s == 1
    depth = w_in.shape[0]
    q_rank = g_qa.shape[1]
    kv_rank = g_kva.shape[1]
    nope = g_qn.shape[1]
    rope = g_qr.shape[1]
    heads = w_kb.shape[1]
    mla_v = w_vb.shape[3]
    dk = g_nq.shape[1]
    nsa_heads = w_nsa_o.shape[1] // dk
    cmp_len = cmp_pos.shape[2]
    hid = cmp_w1.shape[3]
    page_size = cache_mla_latent.shape[2]
    past_len = page_table.shape[1] * page_size
    wkeep_p = min(WINDOW, t_p)
    mla_scale = (nope + rope) ** -0.5
    assert cmp_len == 2 * CMP_STRIDE and 2 * dk == LANES and 2 * rope == LANES and nope == LANES

    n_p = nb * t_p
    n_s = db * t_s
    m = n_p + n_s
    nqw = nsa_heads * dk

    o_nq = q_rank + kv_rank
    o_ta = o_nq + nqw
    o_tb = o_ta + LANES
    o_tc = o_tb + LANES
    o_td = o_tc + LANES
    o_te = o_td + LANES
    tn_g = min(MM_TN, d_model)
    o_ga = -(-(o_te + LANES) // tn_g) * tn_g
    o_gb = o_ga + d_model
    assert 3 * nsa_heads <= LANES and o_gb % tn_g == 0
    widths = [q_rank, kv_rank, rope, nqw] + [dk] * 6 + [3 * nsa_heads, d_model, d_model]
    off = np.cumsum([0] + widths)
    seg = lambda w, i: w[:, int(off[i]):int(off[i + 1])]

    pos_rows = jnp.concatenate([jnp.tile(jnp.arange(t_p, dtype=jnp.int32), nb), jnp.full((n_s,), past_len, jnp.int32)])
    cn, sn = _rope_tables(pos_rows, dk, dk // 4)
    cm, sm = _rope_tables(pos_rows, rope, rope)
    nblk_p = t_p // CMP_STRIDE
    cend_p = jnp.arange(nblk_p, dtype=jnp.int32) * CMP_STRIDE + cmp_len - 1
    cc_p, sc_p = _rope_tables(cend_p, dk, dk // 4)
    lane = jnp.arange(LANES)
    cc_p = jnp.where(lane[None, :] < dk, cc_p, 1.0)
    sc_p = jnp.where(lane[None, :] < dk, sc_p, 0.0)
    cend_s = jnp.arange(past_len // CMP_STRIDE, dtype=jnp.int32) * CMP_STRIDE + cmp_len - 1
    cc_s, sc_s = _rope_tables(cend_s, dk, dk // 4)
    cc_s = jnp.where(lane[None, :] < dk, cc_s, 1.0)
    sc_s = jnp.where(lane[None, :] < dk, sc_s, 0.0)
    pt_flat = page_table.reshape(-1)
    krope_t, kcmp_t, vcmp_t, kslc_t, vslc_t, kwin_t, vwin_t = [
        _token_minor(a) for a in (cache_mla_krope, cache_nsa_kcmp, cache_nsa_vcmp, cache_nsa_kslc, cache_nsa_vslc,
                                  state_nsa_kwin, state_nsa_vwin)]

    gi = jnp.arange(LANES)[:, None]
    hd = jnp.arange(nqw)[None, :] // dk
    expand = jnp.stack([jnp.where(gi == 3 * hd + i, 1.0, 0.0) for i in range(3)]).astype(BF16)

    x = jnp.concatenate([x_prompt.reshape(n_p, d_model), x_sample.reshape(n_s, d_model)], axis=0)
    p_rows, s_rows = [], []
    tk_mla = _pick_tile(t_p, 512, LANES)

    for l in range(depth):
        wl = w_in[l]
        zc = jnp.zeros((d_model, dk), F32)
        w_perm = jnp.concatenate(
            [seg(wl, 0), seg(wl, 1), seg(wl, 3), seg(wl, 6), seg(wl, 8), seg(wl, 2), zc, seg(wl, 4), seg(wl, 5),
             seg(wl, 7), seg(wl, 9), seg(wl, 10), jnp.zeros((d_model, o_ga - o_te - 3 * nsa_heads), F32),
             seg(wl, 11), seg(wl, 12)], axis=1)
        wq = w_qb[l].reshape(q_rank, heads, nope + rope)
        wq_perm = jnp.concatenate([wq[..., :nope].reshape(q_rank, heads * nope),
                                   wq[..., nope:].reshape(q_rank, heads * rope)], axis=1)
        wkb_t = jnp.transpose(w_kb[l], (0, 2, 1)).reshape(heads * nope, kv_rank).astype(BF16)

        h = rms_norm_rows(x, g_attn[l])
        z = matmul(h, w_perm)
        g_a = jnp.concatenate([g_nks[l], g_nkw[l]]).reshape(1, LANES)
        g_b = jnp.concatenate([g_kr[l], jnp.ones((LANES - rope,), F32)]).reshape(1, LANES)
        qa, c, nq, ta, tb = post_in_proj(
            z, g_qa[l].reshape(1, -1), g_kva[l].reshape(1, -1), jnp.tile(g_nq[l], nsa_heads).reshape(1, -1),
            g_a, g_b, cn, sn, cm, sm, q_rank=q_rank, kv_rank=kv_rank, nsa_heads=nsa_heads, dk=dk, mla_rope=rope)
        q = matmul(qa, wq_perm, tn=1024)
        q_abs, q_rope = post_q_proj(
            q, (g_qn[l] * g_kn[l]).reshape(1, -1), jnp.tile(g_qr[l], LANES // rope).reshape(1, -1), w_kb, cm, sm,
            layer=l, heads=heads, nope=nope, rope=rope, kv_rank=kv_rank, scale=mla_scale)

        r_p = prompt_key_scale(c, wkb_t, nb=nb, t=t_p, heads=heads, nope=nope, tk=tk_mla)
        o_lat = prompt_attention(q_abs, c, c, mode="causal", nb=nb, t=t_p, tk=tk_mla, d1=kv_rank, dv=kv_rank,
                                 q2=q_rope, k2=tb, d2=rope, r=r_p)
        cmpw = pack_compress_weights(cmp_pos[l], cmp_w1[l], cmp_b1[l], cmp_w2[l], g_nkc[l])
        kcvc_p = compress_prompt(z, cmpw + (cc_p, sc_p), nb=nb, t=t_p, col_block=o_tc // LANES, dk=dk, hid=hid)
        o_c, sel = prompt_cmp_attention(nq, kcvc_p, nb=nb, t=t_p, dk=dk, cmp_len=cmp_len)
        o_s = prompt_attention(nq, ta, z, mode="select", nb=nb, t=t_p, tk=tk_mla, d1=dk, dv=dk, sel=sel,
                               v_col_block=o_td // LANES)
        o_w = prompt_window_attention(nq, ta, z, nb=nb, t=t_p, d=dk, k_off=dk, v_off=dk, v_col_block=o_td // LANES)

        sl_s = slice(n_p, m)
        c_s = c[sl_s][:, None, :]
        kr_s = tb[sl_s, :rope][:, None, :]
        kcr_s = z[sl_s, o_tc:o_tc + dk][:, None, :]
        vcr_s = z[sl_s, o_tc + dk:o_tc + 2 * dk][:, None, :]
        ks_s = ta[sl_s, :dk][:, None, :]
        kw_s = ta[sl_s, dk:][:, None, :]
        vs_s = z[sl_s, o_td:o_td + dk][:, None, :]
        vw_s = z[sl_s, o_td + dk:o_td + 2 * dk][:, None, :]
        ng_s = z[sl_s, o_te:o_te + 3 * nsa_heads].reshape(n_s, nsa_heads, 3)
        qabs_s = jnp.transpose(q_abs[:, sl_s], (1, 0, 2))
        qrope_s = jnp.transpose(q_rope[:, sl_s], (1, 0, 2))
        nq_s = jnp.transpose(nq[:, sl_s], (1, 0, 2))
        olat_s = sample_mla(pt_flat, cache_mla_latent, krope_t, qabs_s, qrope_s, c_s, tb[sl_s][:, None, :],
                            wkb_t, layer=l, heads=heads, nope=nope)
        oc_s, sel_s = sample_compress_attend(pt_flat, kcmp_t, vcmp_t, nq_s, cmpw + (cc_s, sc_s),
                                             layer=l, cmp_len=cmp_len, hid=hid, past_len=past_len)
        onsa_s = sample_select_window(pt_flat, sel_s.reshape(-1), kslc_t, vslc_t, kwin_t, vwin_t, nq_s,
                                      ta[sl_s][:, None, :], z[sl_s, o_td:o_td + LANES][:, None, :], oc_s, ng_s,
                                      layer=l)
        kw_all = jnp.concatenate([state_nsa_kwin[l], kw_s], axis=1)
        vw_all = jnp.concatenate([state_nsa_vwin[l], vw_s], axis=1)

        o_lat = lax.dynamic_update_slice(o_lat, jnp.transpose(olat_s, (1, 0, 2)), (0, n_p, 0))
        o_mla = head_matmul(o_lat, w_vb, layer=l)
        o_nsa = nsa_gate_combine(o_c, o_s, o_w, z, expand, gate_col_block=o_te // LANES)
        o_nsa = lax.dynamic_update_slice(o_nsa, onsa_s.reshape(n_s, nqw), (n_p, 0))
        y1 = matmul(o_mla, w_mla_o, layer=l, epilogue="gate", extras=(z,), extra_col_blocks=(o_ga // tn_g,), tn=tn_g)
        y = matmul(o_nsa, w_nsa_o, layer=l, epilogue="gate_add", extras=(z, y1), extra_col_blocks=(o_gb // tn_g, 0),
                   out_dtype=BF16, tn=tn_g)
        x = matmul(y, w_out, layer=l, epilogue="add", extras=(x,))
        h2 = rms_norm_rows(x, g_mlp[l])
        u = matmul(h2, w_up, layer=l, epilogue="relu2", out_dtype=BF16)
        x = matmul(u, w_down, layer=l, epilogue="add", extras=(x,), tn=2 * MM_TN, tk=MM_TK // 2)

        pr = lambda a: a[:n_p].reshape(nb, t_p, -1)
        kw_p = pr(ta[:, dk:])
        vw_p = pr(z[:, o_td + dk:o_td + 2 * dk])
        p_rows.append((pr(c), pr(tb[:, :rope]), pr(z[:, o_tc:o_tc + dk]), pr(z[:, o_tc + dk:o_tc + 2 * dk]),
                       pr(ta[:, :dk]), pr(z[:, o_td:o_td + dk]), kw_p[:, t_p - wkeep_p:], vw_p[:, t_p - wkeep_p:]))
        s_rows.append((c_s, kr_s, kcr_s, vcr_s, ks_s, vs_s, kw_all[:, t_s:], vw_all[:, t_s:]))

    xp = x[:n_p].reshape(nb, t_p, d_model)
    xs = x[n_p:].reshape(db, t_s, d_model)
    (lat_p, krope_p, kcmp_p, vcmp_p, kslc_p, vslc_p, kwin_p, vwin_p) = [jnp.stack(a, axis=0) for a in zip(*p_rows)]
    (lat_s, krope_s, kcmp_s, vcmp_s, kslc_s, vslc_s, kwin_s, vwin_s) = [jnp.stack(a, axis=0) for a in zip(*s_rows)]
    return (xp, xs, lat_p, lat_s, krope_p, krope_s, kcmp_p, kcmp_s, vcmp_p, vcmp_s,
            kslc_p, kslc_s, vslc_p, vslc_s, kwin_p, kwin_s, vwin_p, vwin_s)
```

```python
import functools

import jax
import jax.numpy as jnp
import numpy as np
from jax import lax
from jax.experimental import pallas as pl
from jax.experimental.pallas import tpu as pltpu

CMP_STRIDE = 16
SLC_BLOCK = 64
N_SELECT = 16
WINDOW = 512
ROPE_THETA = 500000.0
EPS = 1e-6
NEG = -1e30
FORCE_SCORE = 1e9

F32 = jnp.float32
BF16 = jnp.bfloat16
LANES = 128
ATT_TQ = 128
KEY_SCALE_CHUNK = 256

VMEM_LIMIT_BYTES = 56 * 1024 * 1024

_NT = (((1,), (1,)), ((), ()))


def _cparams(*sem):
    return pltpu.CompilerParams(dimension_semantics=sem, vmem_limit_bytes=VMEM_LIMIT_BYTES)


def _pick_tile(dim, target, quantum):
    best = None
    t = quantum
    while t <= min(dim, target):
        if dim % t == 0:
            best = t
        t += quantum
    return best if best is not None else dim


def _mm_body(*refs, nk, epilogue, n_extra):
    a_ref, b_ref = refs[0], refs[1]
    extras = refs[2:2 + n_extra]
    o_ref = refs[2 + n_extra]
    acc_ref = refs[3 + n_extra] if nk > 1 else None

    part = jnp.dot(a_ref[...].astype(BF16), b_ref[...].astype(BF16), preferred_element_type=F32)

    def finish(acc):
        if epilogue == "none":
            r = acc
        elif epilogue == "relu2":
            r = jnp.square(jnp.maximum(acc, 0.0))
        elif epilogue == "add":
            r = acc + extras[0][...]
        elif epilogue == "gate":
            r = jax.nn.sigmoid(extras[0][...]) * acc
        elif epilogue == "gate_add":
            r = jax.nn.sigmoid(extras[0][...]) * acc + extras[1][...]
        else:
            raise ValueError(epilogue)
        o_ref[...] = r.astype(o_ref.dtype)

    if nk == 1:
        finish(part)
        return

    k = pl.program_id(2)

    @pl.when(k == 0)
    def _():
        acc_ref[...] = part

    @pl.when(k > 0)
    def _():
        acc_ref[...] += part

    @pl.when(k == nk - 1)
    def _():
        finish(acc_ref[...])


MM_TM, MM_TN, MM_TK = 1664, 256, 4096


def matmul(a, b, *, layer=None, epilogue="none", extras=(), extra_col_blocks=None, out_dtype=F32, tm=MM_TM, tn=MM_TN,
           tk=MM_TK):
    m, kdim = a.shape
    kdim2, n = b.shape[-2:]
    assert kdim == kdim2 and (b.ndim == 2) == (layer is None)
    tm = _pick_tile(m, tm, LANES)
    tk = _pick_tile(kdim, tk, LANES)
    tn = min(tn, n)
    nk = kdim // tk
    grid = (m // tm, pl.cdiv(n, tn), nk)
    if layer is None:
        b_spec = pl.BlockSpec((tk, tn), lambda i, j, k: (k, j))
    else:
        b_spec = pl.BlockSpec((None, tk, tn), lambda i, j, k: (layer, k, j))
    in_specs = [pl.BlockSpec((tm, tk), lambda i, j, k: (i, k)), b_spec]
    offs = extra_col_blocks or (0,) * len(extras)
    for o in offs:
        in_specs.append(pl.BlockSpec((tm, tn), lambda i, j, k, o=o: (i, j + o)))
    scratch = [pltpu.VMEM((tm, tn), F32)] if nk > 1 else []
    return pl.pallas_call(
        functools.partial(_mm_body, nk=nk, epilogue=epilogue, n_extra=len(extras)),
        grid=grid,
        in_specs=in_specs,
        out_specs=pl.BlockSpec((tm, tn), lambda i, j, k: (i, j)),
        out_shape=jax.ShapeDtypeStruct((m, n), out_dtype),
        scratch_shapes=scratch,
        compiler_params=_cparams("parallel", "parallel", "arbitrary"),
        name="mm_" + epilogue,
    )(a, b, *extras)


def _group_mean_sq(x, group):
    w = x.shape[1]
    xx = x * x
    hi = xx.astype(BF16)
    lo = (xx - hi.astype(F32)).astype(BF16)
    r = lax.broadcasted_iota(jnp.int32, (LANES, LANES), 0) // group
    c = lax.broadcasted_iota(jnp.int32, (LANES, LANES), 1) // group
    bd = jnp.where(r == c, 1.0, 0.0).astype(BF16)
    outs = []
    for j in range(w // LANES):
        sl = slice(j * LANES, (j + 1) * LANES)
        outs.append(jnp.dot(hi[:, sl], bd, preferred_element_type=F32)
                    + jnp.dot(lo[:, sl], bd, preferred_element_type=F32))
    ms = outs[0] if len(outs) == 1 else jnp.concatenate(outs, axis=1)
    return ms * (1.0 / group)


def _rope128(y, cos, sin, half, group):
    lane = lax.broadcasted_iota(jnp.int32, y.shape, 1) % group
    partner = jnp.where(lane < half, pltpu.roll(y, LANES - half, 1), pltpu.roll(y, half, 1))
    return y * cos + partner * sin


def _row_rms(x):
    return lax.rsqrt(jnp.mean(x * x, axis=-1, keepdims=True) + EPS)


def _p1_body(z_ref, gqa_ref, gkva_ref, gnq_ref, ga_ref, gb_ref, cn_ref, sn_ref, cm_ref, sm_ref,
             qa_ref, c_ref, nq_ref, ta_ref, tb_ref, *, q_rank, kv_rank, nqw, dk, nsa_half, mla_half, nsa_scale):
    o1 = q_rank + kv_rank
    x = z_ref[:, 0:q_rank]
    qa_ref[...] = (x * _row_rms(x) * gqa_ref[...]).astype(qa_ref.dtype)
    x = z_ref[:, q_rank:o1]
    c_ref[...] = x * _row_rms(x) * gkva_ref[...]
    cn, sn = cn_ref[...], sn_ref[...]
    heads_per_tile = LANES // dk
    for j in range(nqw // LANES):
        x = z_ref[:, o1 + j * LANES:o1 + (j + 1) * LANES]
        y = x * lax.rsqrt(_group_mean_sq(x, dk) + EPS) * gnq_ref[:, j * LANES:(j + 1) * LANES]
        y = _rope128(y, cn, sn, nsa_half, dk) * nsa_scale
        for hh in range(heads_per_tile):
            nq_ref[j * heads_per_tile + hh] = y[:, hh * dk:(hh + 1) * dk].astype(nq_ref.dtype)
    o2 = o1 + nqw
    x = z_ref[:, o2:o2 + LANES]
    y = x * lax.rsqrt(_group_mean_sq(x, dk) + EPS) * ga_ref[...]
    ta_ref[...] = _rope128(y, cn, sn, nsa_half, dk)
    x = z_ref[:, o2 + LANES:o2 + 2 * LANES]
    y = x * lax.rsqrt(_group_mean_sq(x, 2 * mla_half) + EPS) * gb_ref[...]
    tb_ref[...] = _rope128(y, cm_ref[...], sm_ref[...], mla_half, 2 * mla_half)


def post_in_proj(z, gqa, gkva, gnq_t, g_a, g_b, cn, sn, cm, sm, *, q_rank, kv_rank, nsa_heads, dk, mla_rope):
    m = z.shape[0]
    nqw = nsa_heads * dk
    win = q_rank + kv_rank + nqw + 2 * LANES
    tm = _pick_tile(m, 640, LANES)
    row = lambda i: (i, 0)
    cst = lambda i: (0, 0)
    return pl.pallas_call(
        functools.partial(_p1_body, q_rank=q_rank, kv_rank=kv_rank, nqw=nqw, dk=dk, nsa_half=dk // 8,
                          mla_half=mla_rope // 2, nsa_scale=dk ** -0.5),
        grid=(m // tm,),
        in_specs=[pl.BlockSpec((tm, win), row),
                  pl.BlockSpec((1, q_rank), cst), pl.BlockSpec((1, kv_rank), cst), pl.BlockSpec((1, nqw), cst),
                  pl.BlockSpec((1, LANES), cst), pl.BlockSpec((1, LANES), cst),
                  pl.BlockSpec((tm, LANES), row), pl.BlockSpec((tm, LANES), row),
                  pl.BlockSpec((tm, LANES), row), pl.BlockSpec((tm, LANES), row)],
        out_specs=[pl.BlockSpec((tm, q_rank), row), pl.BlockSpec((tm, kv_rank), row),
                   pl.BlockSpec((nsa_heads, tm, dk), lambda i: (0, i, 0)),
                   pl.BlockSpec((tm, LANES), row), pl.BlockSpec((tm, LANES), row)],
        out_shape=[jax.ShapeDtypeStruct((m, q_rank), BF16), jax.ShapeDtypeStruct((m, kv_rank), F32),
                   jax.ShapeDtypeStruct((nsa_heads, m, dk), BF16),
                   jax.ShapeDtypeStruct((m, LANES), F32), jax.ShapeDtypeStruct((m, LANES), F32)],
        compiler_params=_cparams("parallel"),
        name="post_in_proj",
    )(z, gqa, gkva, gnq_t, g_a, g_b, cn, sn, cm, sm)


def _p2_body(q_ref, gn_ref, gr_ref, wkb_ref, cm_ref, sm_ref, qabs_ref, qr_ref, *, heads, nope, rope, scale):
    for h in range(heads):
        x = q_ref[:, h * nope:(h + 1) * nope]
        y = x * lax.rsqrt(_group_mean_sq(x, nope) + EPS) * gn_ref[...] * scale
        qa = lax.dot_general(y.astype(BF16), wkb_ref[h].astype(BF16), _NT, preferred_element_type=F32)
        qabs_ref[h] = qa.astype(qabs_ref.dtype)
    o = heads * nope
    per_tile = LANES // rope
    cm, sm = cm_ref[...], sm_ref[...]
    for j in range(heads * rope // LANES):
        x = q_ref[:, o + j * LANES:o + (j + 1) * LANES]
        y = x * lax.rsqrt(_group_mean_sq(x, rope) + EPS) * gr_ref[...]
        y = _rope128(y, cm, sm, rope // 2, rope) * scale
        for hh in range(per_tile):
            qr_ref[j * per_tile + hh] = y[:, hh * rope:(hh + 1) * rope].astype(qr_ref.dtype)


def post_q_proj(q, gn, gr, w_kb, cm, sm, *, layer, heads, nope, rope, kv_rank, scale):
    m = q.shape[0]
    tm = _pick_tile(m, 640, LANES)
    row = lambda i: (i, 0)
    cst = lambda i: (0, 0)
    return pl.pallas_call(
        functools.partial(_p2_body, heads=heads, nope=nope, rope=rope, scale=scale),
        grid=(m // tm,),
        in_specs=[pl.BlockSpec((tm, heads * (nope + rope)), row),
                  pl.BlockSpec((1, nope), cst), pl.BlockSpec((1, LANES), cst),
                  pl.BlockSpec((None, heads, kv_rank, nope), lambda i: (layer, 0, 0, 0)),
                  pl.BlockSpec((tm, LANES), row), pl.BlockSpec((tm, LANES), row)],
        out_specs=[pl.BlockSpec((heads, tm, kv_rank), lambda i: (0, i, 0)),
                   pl.BlockSpec((heads, tm, rope), lambda i: (0, i, 0))],
        out_shape=[jax.ShapeDtypeStruct((heads, m, kv_rank), BF16), jax.ShapeDtypeStruct((heads, m, rope), BF16)],
        compiler_params=_cparams("parallel"),
        name="post_q_proj",
    )(q, gn, gr, w_kb, cm, sm)


def _key_scale(c_bf, wt_bf, heads, nope):
    n = c_bf.shape[0]
    step = min(n, KEY_SCALE_CHUNK)
    parts = []
    for t0 in range(0, n, step):
        kt = lax.dot_general(wt_bf, c_bf[t0:t0 + step], _NT, preferred_element_type=F32)
        kt = kt * kt
        parts.append(jnp.sum(kt.reshape(heads, nope, step), axis=1))
    ss = parts[0] if len(parts) == 1 else jnp.concatenate(parts, axis=1)
    return lax.rsqrt(ss * (1.0 / nope) + EPS)


def _ks_body(c_ref, wt_ref, r_ref, *, heads, nope):
    r_ref[...] = _key_scale(c_ref[...].astype(BF16), wt_ref[...], heads, nope)


def prompt_key_scale(c, wt_bf, *, nb, t, heads, nope, tk):
    kv_rank = c.shape[1]
    nch = t // tk
    return pl.pallas_call(
        functools.partial(_ks_body, heads=heads, nope=nope),
        grid=(nb, nch),
        in_specs=[pl.BlockSpec((tk, kv_rank), lambda b, j: (b * nch + j, 0)),
                  pl.BlockSpec((heads * nope, kv_rank), lambda b, j: (0, 0))],
        out_specs=pl.BlockSpec((None, None, heads, tk), lambda b, j: (b, j, 0, 0)),
        out_shape=jax.ShapeDtypeStruct((nb, nch, heads, tk), F32),
        compiler_params=_cparams("parallel", "parallel"),
        name="prompt_key_scale",
    )(c, wt_bf)


def _att_body(*refs, mode, heads, tq, tk, d1, d2, dv, k1_off, k2_off, v_off, has_r):
    it = iter(refs)
    q1_ref = next(it)
    q2_ref = next(it) if d2 else None
    k1_ref = next(it)
    k2_ref = next(it) if d2 else None
    v_ref = next(it)
    r_ref = next(it) if has_r else None
    sel_ref = next(it) if mode == "select" else None
    o_ref, m_scr, l_scr, acc_scr = next(it), next(it), next(it), next(it)

    i = pl.program_id(1)
    q0 = i * tq
    rows = heads * tq
    q1 = q1_ref[...].reshape(rows, d1)
    q2 = q2_ref[...].reshape(rows, d2) if d2 else None

    m_scr[...] = jnp.full(m_scr.shape, NEG, F32)
    l_scr[...] = jnp.zeros(l_scr.shape, F32)
    acc_scr[...] = jnp.zeros(acc_scr.shape, F32)

    hi = (q0 + tq + tk - 1) // tk

    def chunk(kci, carry):
        ks = pl.multiple_of(kci * tk, tk)
        k1c = k1_ref[pl.ds(ks, tk), k1_off:k1_off + d1].astype(BF16)
        s = lax.dot_general(q1, k1c, _NT, preferred_element_type=F32).reshape(heads, tq, tk)
        if has_r:
            s = s * r_ref[kci][:, None, :]
        if d2:
            k2c = k2_ref[pl.ds(ks, tk), k2_off:k2_off + d2].astype(BF16)
            s = s + lax.dot_general(q2, k2c, _NT, preferred_element_type=F32).reshape(heads, tq, tk)
        qpos = q0 + lax.broadcasted_iota(jnp.int32, (tq, tk), 0)
        kpos = ks + lax.broadcasted_iota(jnp.int32, (tq, tk), 1)
        dist = qpos - kpos
        if mode == "select":
            blk = (ks + lax.broadcasted_iota(jnp.int32, (LANES, tk), 1)) // SLC_BLOCK
            e = jnp.where(blk == lax.broadcasted_iota(jnp.int32, (LANES, tk), 0), 1.0, 0.0).astype(BF16)
            selm = jnp.dot(sel_ref[...], e, preferred_element_type=F32)
            mask = jnp.where(dist >= 0, selm, 0.0) > 0.5
        else:
            mask = dist >= 0
        mask = mask[None]
        s = jnp.where(mask, s, NEG)
        m_prev = m_scr[...]
        m_new = jnp.maximum(m_prev, jnp.max(s, axis=-1, keepdims=True))
        alpha = jnp.exp(m_prev - m_new)
        p = jnp.exp(s - m_new)
        l_scr[...] = alpha * l_scr[...] + jnp.sum(p, axis=-1, keepdims=True)
        vc = v_ref[pl.ds(ks, tk), v_off:v_off + dv].astype(BF16)
        pv = jnp.dot(p.reshape(rows, tk).astype(BF16), vc, preferred_element_type=F32)
        acc_scr[...] = alpha * acc_scr[...] + pv.reshape(heads, tq, dv)
        m_scr[...] = m_new
        return carry

    lax.fori_loop(0, hi, chunk, 0)
    o_ref[...] = (acc_scr[...] / l_scr[...]).astype(o_ref.dtype)


def prompt_attention(q1, k1, v, *, mode, nb, t, tk, d1, dv, k1_off=0, v_off=0, q2=None, k2=None, d2=0, k2_off=0,
                     r=None, sel=None, v_col_block=0):
    heads = q1.shape[0]
    tq = ATT_TQ
    nq = t // tq
    qmap = lambda b, i: (0, b * nq + i, 0)
    kmap = lambda b, i: (b, 0)
    args, specs = [q1], [pl.BlockSpec((heads, tq, d1), qmap)]
    if d2:
        args.append(q2)
        specs.append(pl.BlockSpec((heads, tq, d2), qmap))
    args.append(k1)
    specs.append(pl.BlockSpec((t, k1.shape[1]), kmap))
    if d2:
        args.append(k2)
        specs.append(pl.BlockSpec((t, k2.shape[1]), kmap))
    args.append(v)
    vw = LANES if v.shape[1] > max(LANES, dv) else v.shape[1]
    specs.append(pl.BlockSpec((t, vw), lambda b, i: (b, v_col_block)))
    if r is not None:
        args.append(r)
        specs.append(pl.BlockSpec((None, t // tk, heads, tk), lambda b, i: (b, 0, 0, 0)))
    if sel is not None:
        args.append(sel)
        specs.append(pl.BlockSpec((tq, LANES), lambda b, i: (b * nq + i, 0)))
    return pl.pallas_call(
        functools.partial(_att_body, mode=mode, heads=heads, tq=tq, tk=tk, d1=d1, d2=d2, dv=dv,
                          k1_off=k1_off, k2_off=k2_off, v_off=v_off, has_r=r is not None),
        grid=(nb, nq),
        in_specs=specs,
        out_specs=pl.BlockSpec((heads, tq, dv), qmap),
        out_shape=jax.ShapeDtypeStruct((heads, q1.shape[1], dv), BF16),
        scratch_shapes=[pltpu.VMEM((heads, tq, 1), F32), pltpu.VMEM((heads, tq, 1), F32),
                        pltpu.VMEM((heads, tq, dv), F32)],
        compiler_params=_cparams("parallel", "parallel"),
        name="prompt_att_" + mode,
    )(*args)


def _win_body(q_ref, k_ref, v_ref, o_ref, *, heads, tq, wk, d, k_off, v_off):
    q0 = pl.program_id(1) * tq
    rows = heads * tq
    q = q_ref[...].reshape(rows, d)
    start = pl.multiple_of(jnp.maximum(q0 + tq - wk, 0), tq)
    kc = k_ref[pl.ds(start, wk), k_off:k_off + d].astype(BF16)
    s = lax.dot_general(q, kc, _NT, preferred_element_type=F32).reshape(heads, tq, wk)
    dist = (q0 - start) + lax.broadcasted_iota(jnp.int32, (tq, wk), 0) - lax.broadcasted_iota(jnp.int32, (tq, wk), 1)
    mask = ((dist >= 0) & (dist < WINDOW))[None]
    s = jnp.where(mask, s, NEG)
    p = jnp.exp(s - jnp.max(s, axis=-1, keepdims=True))
    l = jnp.sum(p, axis=-1, keepdims=True)
    vc = v_ref[pl.ds(start, wk), v_off:v_off + d].astype(BF16)
    o = jnp.dot(p.reshape(rows, wk).astype(BF16), vc, preferred_element_type=F32).reshape(heads, tq, d)
    o_ref[...] = (o / l).astype(o_ref.dtype)


def prompt_window_attention(q, k, v, *, nb, t, d, k_off, v_off, v_col_block):
    heads = q.shape[0]
    tq = ATT_TQ
    nq = t // tq
    wk = min(WINDOW + tq, t)
    assert wk % tq == 0
    qmap = lambda b, i: (0, b * nq + i, 0)
    return pl.pallas_call(
        functools.partial(_win_body, heads=heads, tq=tq, wk=wk, d=d, k_off=k_off, v_off=v_off),
        grid=(nb, nq),
        in_specs=[pl.BlockSpec((heads, tq, d), qmap), pl.BlockSpec((t, k.shape[1]), lambda b, i: (b, 0)),
                  pl.BlockSpec((t, LANES), lambda b, i: (b, v_col_block))],
        out_specs=pl.BlockSpec((heads, tq, d), qmap),
        out_shape=jax.ShapeDtypeStruct((heads, q.shape[1], d), BF16),
        compiler_params=_cparams("parallel", "parallel"),
        name="prompt_att_window",
    )(q, k, v)


def _compress_core(rows_ref, nblk, ptop_ref, pbot_ref, wtop_ref, wbot_ref, b1_ref, w2_ref, g_ref, cos_ref, sin_ref,
                   *, dk):
    x = jnp.concatenate([rows_ref[pl.ds(j, nblk, stride=CMP_STRIDE), :] for j in range(CMP_STRIDE)], axis=1)
    top = jnp.dot((x + ptop_ref[...]).astype(BF16), wtop_ref[...], preferred_element_type=F32)
    bot = jnp.dot((x + pbot_ref[...]).astype(BF16), wbot_ref[...], preferred_element_type=F32)
    h = top + pltpu.roll(bot, nblk - 1, 0) + b1_ref[...]
    kv = jnp.dot(jax.nn.gelu(h).astype(BF16), w2_ref[...], preferred_element_type=F32)
    lane = lax.broadcasted_iota(jnp.int32, kv.shape, 1)
    y = jnp.where(lane < dk, kv * lax.rsqrt(_group_mean_sq(kv, dk) + EPS) * g_ref[...], kv)
    return _rope128(y, cos_ref[...], sin_ref[...], dk // 8, dk)


def pack_compress_weights(cmp_pos_l, cmp_w1_l, cmp_b1_l, cmp_w2_l, g_nkc_l):
    _, cmp_len, dk = cmp_pos_l.shape
    hid = cmp_w1_l.shape[2]
    half = cmp_len // 2
    pos = jnp.concatenate([cmp_pos_l[0], cmp_pos_l[1]], axis=1)
    w1k = cmp_w1_l[0].reshape(cmp_len, dk, hid)
    w1v = cmp_w1_l[1].reshape(cmp_len, dk, hid)
    zeros = jnp.zeros_like(w1k)
    w1 = jnp.concatenate([jnp.concatenate([w1k, zeros], axis=2), jnp.concatenate([zeros, w1v], axis=2)], axis=1)
    zk = jnp.zeros((hid, dk), F32)
    w2 = jnp.concatenate([jnp.concatenate([cmp_w2_l[0], zk], axis=1), jnp.concatenate([zk, cmp_w2_l[1]], axis=1)], axis=0)
    return (pos[:half].reshape(1, -1), pos[half:].reshape(1, -1),
            w1[:half].reshape(half * 2 * dk, 2 * hid).astype(BF16), w1[half:].reshape(half * 2 * dk, 2 * hid).astype(BF16),
            cmp_b1_l.reshape(1, 2 * hid), w2.astype(BF16),
            jnp.concatenate([g_nkc_l, jnp.ones((LANES - dk,), F32)]).reshape(1, LANES))


def _cmp_prompt_body(rows_ref, ptop_ref, pbot_ref, wtop_ref, wbot_ref, b1_ref, w2_ref, g_ref, cos_ref, sin_ref, o_ref,
                     *, nblk, dk):
    o_ref[...] = _compress_core(rows_ref, nblk, ptop_ref, pbot_ref, wtop_ref, wbot_ref, b1_ref, w2_ref, g_ref,
                                cos_ref, sin_ref, dk=dk)


def _cmp_weight_specs(dk, hid, nblk):
    z2 = (lambda *a: (0, 0))
    wide = CMP_STRIDE * 2 * dk
    return [pl.BlockSpec((1, wide), z2), pl.BlockSpec((1, wide), z2),
            pl.BlockSpec((wide, 2 * hid), z2), pl.BlockSpec((wide, 2 * hid), z2),
            pl.BlockSpec((1, 2 * hid), z2), pl.BlockSpec((2 * hid, LANES), z2), pl.BlockSpec((1, LANES), z2),
            pl.BlockSpec((nblk, LANES), z2), pl.BlockSpec((nblk, LANES), z2)]


def compress_prompt(z, cmpw, *, nb, t, col_block, dk, hid):
    nblk = t // CMP_STRIDE
    return pl.pallas_call(
        functools.partial(_cmp_prompt_body, nblk=nblk, dk=dk),
        grid=(nb,),
        in_specs=[pl.BlockSpec((t, LANES), lambda b: (b, col_block))] + _cmp_weight_specs(dk, hid, nblk),
        out_specs=pl.BlockSpec((None, nblk, LANES), lambda b: (b, 0, 0)),
        out_shape=jax.ShapeDtypeStruct((nb, nblk, LANES), F32),
        compiler_params=_cparams("parallel"),
        name="compress_prompt",
    )(z, *cmpw)


def _select_rank(imp, qpos, ns):
    lane = lax.broadcasted_iota(jnp.int32, imp.shape, 1)
    cur = qpos // SLC_BLOCK
    forced = (lane == 0) | (lane == cur) | (lane == cur - 1)
    valid = lane * SLC_BLOCK <= qpos
    sc = jnp.where(forced, FORCE_SCORE, jnp.where(valid, imp, -FORCE_SCORE))
    sc = jnp.where(lane < ns, sc, -2.0 * FORCE_SCORE)
    rank = jnp.zeros(imp.shape, F32)
    for s in range(ns):
        col = sc[:, s:s + 1]
        beats = (col > sc) | ((col == sc) & (lane > s))
        rank = rank + jnp.where(beats, 1.0, 0.0)
    return rank, lane


def _select_mask(imp, qpos, ns):
    rank, lane = _select_rank(imp, qpos, ns)
    return jnp.where((rank < float(min(N_SELECT, ns))) & (lane < ns), 1.0, 0.0)


def _select_indices(imp, qpos, ns):
    rank, lane = _select_rank(imp, qpos, ns)
    lanef = lane.astype(F32)
    out_lane = lax.broadcasted_iota(jnp.int32, (imp.shape[0], LANES), 1)
    out = jnp.zeros((imp.shape[0], LANES), F32)
    for j in range(N_SELECT):
        idx = jnp.sum(jnp.where((rank == float(j)) & (lane < ns), lanef, 0.0), axis=-1, keepdims=True)
        out = jnp.where(out_lane == j, idx, out)
    return out.astype(jnp.int32)


def _overlap_matrix(ncp, cmp_len, width=LANES):
    n = lax.broadcasted_iota(jnp.int32, (ncp, width), 0)
    s = lax.broadcasted_iota(jnp.int32, (ncp, width), 1)
    cstart = n * CMP_STRIDE
    cend = cstart + cmp_len - 1
    bstart = s * SLC_BLOCK
    return jnp.where((cstart < bstart + SLC_BLOCK) & (cend >= bstart), 1.0, 0.0).astype(BF16)


def _cmpatt_body(q_ref, kv_ref, o_ref, sel_ref, *, heads, tq, dk, nc, cmp_len, ns):
    i = pl.program_id(1)
    rows = heads * tq
    ncp = kv_ref.shape[0]
    q = q_ref[...].reshape(rows, dk)
    kv = kv_ref[...]
    kc = kv[:, :dk].astype(BF16)
    vc = kv[:, dk:2 * dk].astype(BF16)
    s = lax.dot_general(q, kc, _NT, preferred_element_type=F32).reshape(heads, tq, ncp)
    qpos = i * tq + lax.broadcasted_iota(jnp.int32, (tq, ncp), 0)
    n = lax.broadcasted_iota(jnp.int32, (tq, ncp), 1)
    mask = ((n * CMP_STRIDE + cmp_len - 1 <= qpos) & (n < nc))[None]
    s = jnp.where(mask, s, NEG)
    m = jnp.max(s, axis=-1, keepdims=True)
    e = jnp.where(mask, jnp.exp(s - m), 0.0)
    p = e / jnp.maximum(jnp.sum(e, axis=-1, keepdims=True), 1e-30)
    o = jnp.dot(p.reshape(rows, ncp).astype(BF16), vc, preferred_element_type=F32)
    o_ref[...] = o.reshape(heads, tq, dk).astype(o_ref.dtype)
    psum = jnp.sum(p, axis=0)
    hi = psum.astype(BF16)
    lo = (psum - hi.astype(F32)).astype(BF16)
    ov = _overlap_matrix(ncp, cmp_len)
    imp = jnp.dot(hi, ov, preferred_element_type=F32) + jnp.dot(lo, ov, preferred_element_type=F32)
    qp = i * tq + lax.broadcasted_iota(jnp.int32, (tq, 1), 0)
    sel_ref[...] = _select_mask(imp, qp, ns).astype(sel_ref.dtype)


def prompt_cmp_attention(nq, kcvc, *, nb, t, dk, cmp_len):
    heads, m, _ = nq.shape
    tq = ATT_TQ
    nqb = t // tq
    ncp = kcvc.shape[1]
    nc = (t - cmp_len) // CMP_STRIDE + 1
    ns = -(-t // SLC_BLOCK)
    assert ns <= LANES and ncp <= LANES
    qmap = lambda b, i: (0, b * nqb + i, 0)
    return pl.pallas_call(
        functools.partial(_cmpatt_body, heads=heads, tq=tq, dk=dk, nc=nc, cmp_len=cmp_len, ns=ns),
        grid=(nb, nqb),
        in_specs=[pl.BlockSpec((heads, tq, dk), qmap), pl.BlockSpec((None, ncp, LANES), lambda b, i: (b, 0, 0))],
        out_specs=[pl.BlockSpec((heads, tq, dk), qmap), pl.BlockSpec((tq, LANES), lambda b, i: (b * nqb + i, 0))],
        out_shape=[jax.ShapeDtypeStruct((heads, m, dk), BF16), jax.ShapeDtypeStruct((m, LANES), BF16)],
        compiler_params=_cparams("parallel", "parallel"),
        name="prompt_cmp_att",
    )(nq, kcvc)


def _gate_body(oc_ref, os_ref, ow_ref, g_ref, e_ref, o_ref, *, heads, dk):
    sig = jax.nn.sigmoid(g_ref[...])
    hi = sig.astype(BF16)
    lo = (sig - hi.astype(F32)).astype(BF16)
    gates = [jnp.dot(hi, e_ref[i], preferred_element_type=F32) + jnp.dot(lo, e_ref[i], preferred_element_type=F32)
             for i in range(3)]
    for h in range(heads):
        sl = slice(h * dk, (h + 1) * dk)
        o = (gates[0][:, sl] * oc_ref[h].astype(F32) + gates[1][:, sl] * os_ref[h].astype(F32)
             + gates[2][:, sl] * ow_ref[h].astype(F32))
        o_ref[:, sl] = o.astype(o_ref.dtype)


def nsa_gate_combine(o_c, o_s, o_w, z, expand, *, gate_col_block):
    heads, m, dk = o_c.shape
    tm = _pick_tile(m, 640, LANES)
    hmap = lambda i: (0, i, 0)
    return pl.pallas_call(
        functools.partial(_gate_body, heads=heads, dk=dk),
        grid=(m // tm,),
        in_specs=[pl.BlockSpec((heads, tm, dk), hmap)] * 3
        + [pl.BlockSpec((tm, LANES), lambda i: (i, gate_col_block)),
           pl.BlockSpec((3, LANES, heads * dk), lambda i: (0, 0, 0))],
        out_specs=pl.BlockSpec((tm, heads * dk), lambda i: (i, 0)),
        out_shape=jax.ShapeDtypeStruct((m, heads * dk), BF16),
        compiler_params=_cparams("parallel"),
        name="nsa_gate_combine",
    )(o_c, o_s, o_w, z, expand)


def _headmm_body(x_ref, w_ref, o_ref):
    o_ref[...] = jnp.dot(x_ref[...], w_ref[...].astype(BF16), preferred_element_type=F32).astype(o_ref.dtype)


def head_matmul(x, w, *, layer):
    heads, m, k = x.shape
    n = w.shape[3]
    tm = _pick_tile(m, 1664, LANES)
    return pl.pallas_call(
        _headmm_body,
        grid=(m // tm, heads),
        in_specs=[pl.BlockSpec((None, tm, k), lambda i, h: (h, i, 0)),
                  pl.BlockSpec((None, None, k, n), lambda i, h: (layer, h, 0, 0))],
        out_specs=pl.BlockSpec((tm, n), lambda i, h: (i, h)),
        out_shape=jax.ShapeDtypeStruct((m, heads * n), BF16),
        compiler_params=_cparams("parallel", "parallel"),
        name="head_matmul",
    )(x, w)


def _rms_body(x_ref, g_ref, o_ref):
    x = x_ref[...]
    o_ref[...] = (x * _row_rms(x) * g_ref[...]).astype(o_ref.dtype)


def rms_norm_rows(x, g):
    m, d = x.shape
    tm = _pick_tile(m, 640, LANES)
    return pl.pallas_call(
        _rms_body,
        grid=(m // tm,),
        in_specs=[pl.BlockSpec((tm, d), lambda i: (i, 0)), pl.BlockSpec((1, d), lambda i: (0, 0))],
        out_specs=pl.BlockSpec((tm, d), lambda i: (i, 0)),
        out_shape=jax.ShapeDtypeStruct((m, d), BF16),
        compiler_params=_cparams("parallel"),
        name="rms_norm_rows",
    )(x, g.reshape(1, d))


PAGES_PER_STEP = 16


def _page_specs(layer, tile, n_pages, gp):
    def mk(i):
        return pl.BlockSpec((None, None) + tile, lambda b, g, pt: (layer, pt[b * n_pages + g * gp + i], 0, 0))
    return [mk(i) for i in range(gp)]


def _cat_pages(refs, axis=0):
    return jnp.concatenate([r[...] for r in refs], axis=axis) if len(refs) > 1 else refs[0][...]


def _token_minor(pool):
    return jnp.swapaxes(pool, 2, 3)


def _online_update(m_scr, l_scr, acc_scr, s, v_bf, mask=None):
    if mask is not None:
        s = jnp.where(mask, s, NEG)
    m_prev = m_scr[...]
    m_new = jnp.maximum(m_prev, jnp.max(s, axis=-1, keepdims=True))
    alpha = jnp.exp(m_prev - m_new)
    p = jnp.exp(s - m_new)
    if mask is not None:
        p = jnp.where(mask, p, 0.0)
    l_scr[...] = alpha * l_scr[...] + jnp.sum(p, axis=-1, keepdims=True)
    acc_scr[...] = alpha * acc_scr[...] + jnp.dot(p.astype(BF16), v_bf, preferred_element_type=F32)
    m_scr[...] = m_new


def _smla_body(pt_ref, *refs, gp, heads, nope, rope):
    del pt_ref
    lat = refs[:gp]
    krp = refs[gp:2 * gp]
    qa_ref, qr_ref, cs_ref, krs_ref, wt_ref, o_ref, m_scr, l_scr, acc_scr = refs[2 * gp:]
    g = pl.program_id(1)

    @pl.when(g == 0)
    def _():
        m_scr[...] = jnp.full(m_scr.shape, NEG, F32)
        l_scr[...] = jnp.zeros(l_scr.shape, F32)
        acc_scr[...] = jnp.zeros(acc_scr.shape, F32)

    qa, qr, wt = qa_ref[...], qr_ref[...], wt_ref[...]

    def attend(c_bf, s_rope, mask):
        r = _key_scale(c_bf, wt, heads, nope)
        s = lax.dot_general(qa, c_bf, _NT, preferred_element_type=F32) * r + s_rope
        _online_update(m_scr, l_scr, acc_scr, s, c_bf, mask)

    kr_t = _cat_pages(krp, axis=1).astype(BF16)
    attend(_cat_pages(lat).astype(BF16), jnp.dot(qr, kr_t, preferred_element_type=F32), None)

    @pl.when(g == pl.num_programs(1) - 1)
    def _():
        c_new = jnp.broadcast_to(cs_ref[...], (LANES, cs_ref.shape[1])).astype(BF16)
        s_rope = jnp.sum(qr.astype(F32) * krs_ref[...][:, :rope], axis=-1, keepdims=True)
        attend(c_new, s_rope, lax.broadcasted_iota(jnp.int32, (heads, LANES), 1) == 0)
        o_ref[...] = (acc_scr[...] / l_scr[...]).astype(o_ref.dtype)


def sample_mla(page_table_flat, lat_pool, krope_pool_t, qabs_s, qrope_s, c_new, tb_new, wt_bf, *, layer, heads, nope):
    db, _, kv_rank = qabs_s.shape
    rope = qrope_s.shape[2]
    page_size = lat_pool.shape[2]
    n_pages = page_table_flat.shape[0] // db
    gp = min(PAGES_PER_STEP, n_pages)
    assert n_pages % gp == 0
    per_b = lambda w1, w2: pl.BlockSpec((None, w1, w2), lambda b, g, pt: (b, 0, 0))
    grid_spec = pltpu.PrefetchScalarGridSpec(
        num_scalar_prefetch=1,
        grid=(db, n_pages // gp),
        in_specs=_page_specs(layer, (page_size, kv_rank), n_pages, gp) + _page_specs(layer, (rope, page_size), n_pages, gp)
        + [per_b(heads, kv_rank), per_b(heads, rope), per_b(1, kv_rank), per_b(1, tb_new.shape[2]),
           pl.BlockSpec((heads * nope, kv_rank), lambda b, g, pt: (0, 0))],
        out_specs=per_b(heads, kv_rank),
        scratch_shapes=[pltpu.VMEM((heads, 1), F32), pltpu.VMEM((heads, 1), F32), pltpu.VMEM((heads, kv_rank), F32)],
    )
    return pl.pallas_call(
        functools.partial(_smla_body, gp=gp, heads=heads, nope=nope, rope=rope),
        grid_spec=grid_spec,
        out_shape=jax.ShapeDtypeStruct((db, heads, kv_rank), BF16),
        compiler_params=_cparams("parallel", "arbitrary"),
        name="sample_mla",
    )(page_table_flat, *([lat_pool] * gp), *([krope_pool_t] * gp), qabs_s, qrope_s, c_new, tb_new, wt_bf)


def _scmp_body(pt_ref, *refs, gp, heads, dk, page_size, nblk, nc, cmp_len, ns, nsp, past_len):
    del pt_ref
    kpg = refs[:gp]
    vpg = refs[gp:2 * gp]
    (q_ref, ptop_ref, pbot_ref, wtop_ref, wbot_ref, b1_ref, w2_ref, gk_ref, cos_ref, sin_ref,
     oc_ref, sel_ref, rows_scr) = refs[2 * gp:]
    g = pl.program_id(1)
    for i in range(gp):
        row0 = pl.multiple_of((g * gp + i) * page_size, page_size)
        rows_scr[pl.ds(row0, page_size), :] = jnp.concatenate([kpg[i][...], vpg[i][...]], axis=0).T

    @pl.when(g == pl.num_programs(1) - 1)
    def _():
        kv = _compress_core(rows_scr, nblk, ptop_ref, pbot_ref, wtop_ref, wbot_ref, b1_ref, w2_ref, gk_ref,
                            cos_ref, sin_ref, dk=dk)
        kc = kv[:, :dk].astype(BF16)
        vc = kv[:, dk:2 * dk].astype(BF16)
        s = lax.dot_general(q_ref[...], kc, _NT, preferred_element_type=F32)
        n = lax.broadcasted_iota(jnp.int32, s.shape, 1)
        mask = (n * CMP_STRIDE + cmp_len - 1 <= past_len) & (n < nc)
        s = jnp.where(mask, s, NEG)
        mx = jnp.max(s, axis=-1, keepdims=True)
        e = jnp.where(mask, jnp.exp(s - mx), 0.0)
        p = e / jnp.maximum(jnp.sum(e, axis=-1, keepdims=True), 1e-30)
        oc_ref[...] = jnp.dot(p.astype(BF16), vc, preferred_element_type=F32)
        psum = jnp.broadcast_to(jnp.sum(p, axis=0, keepdims=True), (8, nblk))
        hi = psum.astype(BF16)
        lo = (psum - hi.astype(F32)).astype(BF16)
        ov = _overlap_matrix(nblk, cmp_len, nsp)
        imp = jnp.dot(hi, ov, preferred_element_type=F32) + jnp.dot(lo, ov, preferred_element_type=F32)
        sel = _select_indices(imp, jnp.full((8, 1), past_len, jnp.int32), ns)
        sel_ref[...] = sel[0:1, :]


def sample_compress_attend(page_table_flat, kpool, vpool, nq_s, cmpw, *, layer, cmp_len, hid, past_len):
    db, heads, dk = nq_s.shape
    page_size = kpool.shape[3]
    n_pages = page_table_flat.shape[0] // db
    gp = min(PAGES_PER_STEP, n_pages)
    nblk = past_len // CMP_STRIDE
    nc = (past_len + 1 - cmp_len) // CMP_STRIDE + 1
    ns = -(-(past_len + 1) // SLC_BLOCK)
    nsp = -(-ns // LANES) * LANES
    assert n_pages % gp == 0 and nc <= nblk and ns >= N_SELECT and 2 * dk == LANES
    grid_spec = pltpu.PrefetchScalarGridSpec(
        num_scalar_prefetch=1,
        grid=(db, n_pages // gp),
        in_specs=_page_specs(layer, (dk, page_size), n_pages, gp) + _page_specs(layer, (dk, page_size), n_pages, gp)
        + [pl.BlockSpec((None, heads, dk), lambda b, g, pt: (b, 0, 0))] + _cmp_weight_specs(dk, hid, nblk),
        out_specs=[pl.BlockSpec((None, heads, dk), lambda b, g, pt: (b, 0, 0)),
                   pl.BlockSpec((None, 1, LANES), lambda b, g, pt: (b, 0, 0))],
        scratch_shapes=[pltpu.VMEM((past_len, LANES), F32)],
    )
    return pl.pallas_call(
        functools.partial(_scmp_body, gp=gp, heads=heads, dk=dk, page_size=page_size, nblk=nblk, nc=nc,
                          cmp_len=cmp_len, ns=ns, nsp=nsp, past_len=past_len),
        grid_spec=grid_spec,
        out_shape=[jax.ShapeDtypeStruct((db, heads, dk), F32), jax.ShapeDtypeStruct((db, 1, LANES), jnp.int32)],
        compiler_params=_cparams("parallel", "arbitrary"),
        name="sample_compress_attend",
    )(page_table_flat, *([kpool] * gp), *([vpool] * gp), nq_s, *cmpw)


def _sel_page(si, b, j, bpp, n_pages):
    return jnp.clip(si[b * LANES + j] // bpp, 0, n_pages - 1)


def _ssel_body(pt_ref, si_ref, *refs, nsel, heads, dk, page_size, n_pages, w0):
    del pt_ref
    kpg = refs[:nsel]
    vpg = refs[nsel:2 * nsel]
    q_ref, ta_ref, td_ref, kwin_ref, vwin_ref, oc_ref, ng_ref, o_ref = refs[2 * nsel:]
    b = pl.program_id(0)
    bpp = page_size // SLC_BLOCK
    q = q_ref[...]
    qf = q.astype(F32)
    ta = ta_ref[...]
    td = td_ref[...]

    lane = lax.broadcasted_iota(jnp.int32, (1, page_size), 1)
    pieces = []
    for j in range(nsel):
        blk = si_ref[b * LANES + j]
        blk = jnp.where(blk < n_pages * bpp, blk, -1)
        tok_blk = (_sel_page(si_ref, b, j, bpp, n_pages) * page_size + lane) // SLC_BLOCK
        pieces.append(jnp.where(tok_blk == blk, 1.0, 0.0))
    mask = jnp.concatenate(pieces, axis=1) > 0.5
    s = jnp.dot(q, _cat_pages(kpg, axis=1).astype(BF16), preferred_element_type=F32)
    mask = jnp.broadcast_to(mask, s.shape)
    s = jnp.where(mask, s, NEG)
    s_new = jnp.sum(qf * ta[:, :dk], axis=-1, keepdims=True)
    ms = jnp.maximum(jnp.max(s, axis=-1, keepdims=True), s_new)
    p = jnp.where(mask, jnp.exp(s - ms), 0.0)
    p_new = jnp.exp(s_new - ms)
    pv = lax.dot_general(p.astype(BF16), _cat_pages(vpg, axis=1).astype(BF16), _NT, preferred_element_type=F32)
    o_s = (pv + p_new * td[:, :dk]) / (jnp.sum(p, axis=-1, keepdims=True) + p_new)

    sw = jnp.dot(q, kwin_ref[...].astype(BF16), preferred_element_type=F32)
    wmask = lax.broadcasted_iota(jnp.int32, sw.shape, 1) >= w0
    sw = jnp.where(wmask, sw, NEG)
    sw_new = jnp.sum(qf * ta[:, dk:2 * dk], axis=-1, keepdims=True)
    mw = jnp.maximum(jnp.max(sw, axis=-1, keepdims=True), sw_new)
    pw = jnp.where(wmask, jnp.exp(sw - mw), 0.0)
    pw_new = jnp.exp(sw_new - mw)
    o_w = ((lax.dot_general(pw.astype(BF16), vwin_ref[...].astype(BF16), _NT, preferred_element_type=F32)
            + pw_new * td[:, dk:2 * dk]) / (jnp.sum(pw, axis=-1, keepdims=True) + pw_new))
    gate = jax.nn.sigmoid(ng_ref[...])
    o = gate[:, 0:1] * oc_ref[...] + gate[:, 1:2] * o_s + gate[:, 2:3] * o_w
    o_ref[...] = o.astype(o_ref.dtype)


def sample_select_window(page_table_flat, sel_flat, kpool, vpool, kwin, vwin, nq_s, ta_new, td_new, o_c, ngate_s, *,
                         layer):
    db, heads, dk = nq_s.shape
    page_size = kpool.shape[3]
    n_pages = page_table_flat.shape[0] // db
    bpp = page_size // SLC_BLOCK
    wbuf = kwin.shape[3]
    w0 = max(0, wbuf + 1 - WINDOW)
    per_b = lambda w1, w2: pl.BlockSpec((None, w1, w2), lambda b, pt, si: (b, 0, 0))
    win = pl.BlockSpec((None, None, dk, wbuf), lambda b, pt, si: (layer, b, 0, 0))

    def sel_spec(j):
        return pl.BlockSpec((None, None, dk, page_size),
                            lambda b, pt, si: (layer, pt[b * n_pages + _sel_page(si, b, j, bpp, n_pages)], 0, 0))
    sel_specs = [sel_spec(j) for j in range(N_SELECT)]
    grid_spec = pltpu.PrefetchScalarGridSpec(
        num_scalar_prefetch=2,
        grid=(db,),
        in_specs=sel_specs + sel_specs
        + [per_b(heads, dk), per_b(1, LANES), per_b(1, LANES), win, win, per_b(heads, dk), per_b(heads, 3)],
        out_specs=per_b(heads, dk),
    )
    return pl.pallas_call(
        functools.partial(_ssel_body, nsel=N_SELECT, heads=heads, dk=dk, page_size=page_size, n_pages=n_pages, w0=w0),
        grid_spec=grid_spec,
        out_shape=jax.ShapeDtypeStruct((db, heads, dk), BF16),
        compiler_params=_cparams("parallel"),
        name="sample_select_window",
    )(page_table_flat, sel_flat, *([kpool] * N_SELECT), *([vpool] * N_SELECT), nq_s, ta_new, td_new, kwin, vwin, o_c,
      ngate_s)


def _rope_tables(pos, group, rot_dim):
    half = rot_dim // 2
    inv = ROPE_THETA ** (-jnp.arange(half, dtype=F32) / half)
    ang = pos.astype(F32)[:, None] * inv[None, :]
    cos, sin = jnp.cos(ang), jnp.sin(ang)
    ones = jnp.ones((pos.shape[0], group - rot_dim), F32)
    c = jnp.concatenate([cos, cos, ones], axis=1)
    s = jnp.concatenate([-sin, sin, 0.0 * ones], axis=1)
    reps = LANES // group
    return jnp.tile(c, (1, reps)), jnp.tile(s, (1, reps))


def kernel(x_prompt, x_sample, cache_mla_latent, cache_mla_krope, cache_nsa_kcmp, cache_nsa_vcmp,
           cache_nsa_kslc, cache_nsa_vslc, state_nsa_kwin, state_nsa_vwin, page_table,
           g_attn, w_in, g_qa, w_qb, g_kva, g_qn, g_qr, g_kn, g_kr, w_kb, w_vb, w_mla_o,
           g_nq, g_nkc, g_nks, g_nkw, cmp_pos, cmp_w1, cmp_b1, cmp_w2, w_nsa_o,
           w_out, g_mlp, w_up, w_down):
    nb, t_p, d_model = x_prompt.shape
    db, t_s, _ = x_sample.shape
    assert t_s == 1
    depth = w_in.shape[0]
    q_rank = g_qa.shape[1]
    kv_rank = g_kva.shape[1]
    nope = g_qn.shape[1]
    rope = g_qr.shape[1]
    heads = w_kb.shape[1]
    mla_v = w_vb.shape[3]
    dk = g_nq.shape[1]
    nsa_heads = w_nsa_o.shape[1] // dk
    cmp_len = cmp_pos.shape[2]
    hid = cmp_w1.shape[3]
    page_size = cache_mla_latent.shape[2]
    past_len = page_table.shape[1] * page_size
    wkeep_p = min(WINDOW, t_p)
    mla_scale = (nope + rope) ** -0.5
    assert cmp_len == 2 * CMP_STRIDE and 2 * dk == LANES and 2 * rope == LANES and nope == LANES

    n_p = nb * t_p
    n_s = db * t_s
    m = n_p + n_s
    nqw = nsa_heads * dk

    o_nq = q_rank + kv_rank
    o_ta = o_nq + nqw
    o_tb = o_ta + LANES
    o_tc = o_tb + LANES
    o_td = o_tc + LANES
    o_te = o_td + LANES
    tn_g = min(MM_TN, d_model)
    o_ga = -(-(o_te + LANES) // tn_g) * tn_g
    o_gb = o_ga + d_model
    assert 3 * nsa_heads <= LANES and o_gb % tn_g == 0
    widths = [q_rank, kv_rank, rope, nqw] + [dk] * 6 + [3 * nsa_heads, d_model, d_model]
    off = np.cumsum([0] + widths)
    seg = lambda w, i: w[:, int(off[i]):int(off[i + 1])]

    pos_rows = jnp.concatenate([jnp.tile(jnp.arange(t_p, dtype=jnp.int32), nb), jnp.full((n_s,), past_len, jnp.int32)])
    cn, sn = _rope_tables(pos_rows, dk, dk // 4)
    cm, sm = _rope_tables(pos_rows, rope, rope)
    nblk_p = t_p // CMP_STRIDE
    cend_p = jnp.arange(nblk_p, dtype=jnp.int32) * CMP_STRIDE + cmp_len - 1
    cc_p, sc_p = _rope_tables(cend_p, dk, dk // 4)
    lane = jnp.arange(LANES)
    cc_p = jnp.where(lane[None, :] < dk, cc_p, 1.0)
    sc_p = jnp.where(lane[None, :] < dk, sc_p, 0.0)
    cend_s = jnp.arange(past_len // CMP_STRIDE, dtype=jnp.int32) * CMP_STRIDE + cmp_len - 1
    cc_s, sc_s = _rope_tables(cend_s, dk, dk // 4)
    cc_s = jnp.where(lane[None, :] < dk, cc_s, 1.0)
    sc_s = jnp.where(lane[None, :] < dk, sc_s, 0.0)
    pt_flat = page_table.reshape(-1)
    krope_t, kcmp_t, vcmp_t, kslc_t, vslc_t, kwin_t, vwin_t = [
        _token_minor(a) for a in (cache_mla_krope, cache_nsa_kcmp, cache_nsa_vcmp, cache_nsa_kslc, cache_nsa_vslc,
                                  state_nsa_kwin, state_nsa_vwin)]

    gi = jnp.arange(LANES)[:, None]
    hd = jnp.arange(nqw)[None, :] // dk
    expand = jnp.stack([jnp.where(gi == 3 * hd + i, 1.0, 0.0) for i in range(3)]).astype(BF16)

    x = jnp.concatenate([x_prompt.reshape(n_p, d_model), x_sample.reshape(n_s, d_model)], axis=0)
    p_rows, s_rows = [], []
    tk_mla = _pick_tile(t_p, 512, LANES)

    for l in range(depth):
        wl = w_in[l]
        zc = jnp.zeros((d_model, dk), F32)
        w_perm = jnp.concatenate(
            [seg(wl, 0), seg(wl, 1), seg(wl, 3), seg(wl, 6), seg(wl, 8), seg(wl, 2), zc, seg(wl, 4), seg(wl, 5),
             seg(wl, 7), seg(wl, 9), seg(wl, 10), jnp.zeros((d_model, o_ga - o_te - 3 * nsa_heads), F32),
             seg(wl, 11), seg(wl, 12)], axis=1)
        wq = w_qb[l].reshape(q_rank, heads, nope + rope)
        wq_perm = jnp.concatenate([wq[..., :nope].reshape(q_rank, heads * nope),
                                   wq[..., nope:].reshape(q_rank, heads * rope)], axis=1)
        wkb_t = jnp.transpose(w_kb[l], (0, 2, 1)).reshape(heads * nope, kv_rank).astype(BF16)

        h = rms_norm_rows(x, g_attn[l])
        z = matmul(h, w_perm)
        g_a = jnp.concatenate([g_nks[l], g_nkw[l]]).reshape(1, LANES)
        g_b = jnp.concatenate([g_kr[l], jnp.ones((LANES - rope,), F32)]).reshape(1, LANES)
        qa, c, nq, ta, tb = post_in_proj(
            z, g_qa[l].reshape(1, -1), g_kva[l].reshape(1, -1), jnp.tile(g_nq[l], nsa_heads).reshape(1, -1),
            g_a, g_b, cn, sn, cm, sm, q_rank=q_rank, kv_rank=kv_rank, nsa_heads=nsa_heads, dk=dk, mla_rope=rope)
        q = matmul(qa, wq_perm, tn=1024)
        q_abs, q_rope = post_q_proj(
            q, (g_qn[l] * g_kn[l]).reshape(1, -1), jnp.tile(g_qr[l], LANES // rope).reshape(1, -1), w_kb, cm, sm,
            layer=l, heads=heads, nope=nope, rope=rope, kv_rank=kv_rank, scale=mla_scale)

        r_p = prompt_key_scale(c, wkb_t, nb=nb, t=t_p, heads=heads, nope=nope, tk=tk_mla)
        o_lat = prompt_attention(q_abs, c, c, mode="causal", nb=nb, t=t_p, tk=tk_mla, d1=kv_rank, dv=kv_rank,
                                 q2=q_rope, k2=tb, d2=rope, r=r_p)
        cmpw = pack_compress_weights(cmp_pos[l], cmp_w1[l], cmp_b1[l], cmp_w2[l], g_nkc[l])
        kcvc_p = compress_prompt(z, cmpw + (cc_p, sc_p), nb=nb, t=t_p, col_block=o_tc // LANES, dk=dk, hid=hid)
        o_c, sel = prompt_cmp_attention(nq, kcvc_p, nb=nb, t=t_p, dk=dk, cmp_len=cmp_len)
        o_s = prompt_attention(nq, ta, z, mode="select", nb=nb, t=t_p, tk=tk_mla, d1=dk, dv=dk, sel=sel,
                               v_col_block=o_td // LANES)
        o_w = prompt_window_attention(nq, ta, z, nb=nb, t=t_p, d=dk, k_off=dk, v_off=dk, v_col_block=o_td // LANES)

        sl_s = slice(n_p, m)
        c_s = c[sl_s][:, None, :]
        kr_s = tb[sl_s, :rope][:, None, :]
        kcr_s = z[sl_s, o_tc:o_tc + dk][:, None, :]
        vcr_s = z[sl_s, o_tc + dk:o_tc + 2 * dk][:, None, :]
        ks_s = ta[sl_s, :dk][:, None, :]
        kw_s = ta[sl_s, dk:][:, None, :]
        vs_s = z[sl_s, o_td:o_td + dk][:, None, :]
        vw_s = z[sl_s, o_td + dk:o_td + 2 * dk][:, None, :]
        ng_s = z[sl_s, o_te:o_te + 3 * nsa_heads].reshape(n_s, nsa_heads, 3)
        qabs_s = jnp.transpose(q_abs[:, sl_s], (1, 0, 2))
        qrope_s = jnp.transpose(q_rope[:, sl_s], (1, 0, 2))
        nq_s = jnp.transpose(nq[:, sl_s], (1, 0, 2))
        olat_s = sample_mla(pt_flat, cache_mla_latent, krope_t, qabs_s, qrope_s, c_s, tb[sl_s][:, None, :],
                            wkb_t, layer=l, heads=heads, nope=nope)
        oc_s, sel_s = sample_compress_attend(pt_flat, kcmp_t, vcmp_t, nq_s, cmpw + (cc_s, sc_s),
                                             layer=l, cmp_len=cmp_len, hid=hid, past_len=past_len)
        onsa_s = sample_select_window(pt_flat, sel_s.reshape(-1), kslc_t, vslc_t, kwin_t, vwin_t, nq_s,
                                      ta[sl_s][:, None, :], z[sl_s, o_td:o_td + LANES][:, None, :], oc_s, ng_s,
                                      layer=l)
        kw_all = jnp.concatenate([state_nsa_kwin[l], kw_s], axis=1)
        vw_all = jnp.concatenate([state_nsa_vwin[l], vw_s], axis=1)

        o_lat = lax.dynamic_update_slice(o_lat, jnp.transpose(olat_s, (1, 0, 2)), (0, n_p, 0))
        o_mla = head_matmul(o_lat, w_vb, layer=l)
        o_nsa = nsa_gate_combine(o_c, o_s, o_w, z, expand, gate_col_block=o_te // LANES)
        o_nsa = lax.dynamic_update_slice(o_nsa, onsa_s.reshape(n_s, nqw), (n_p, 0))
        y1 = matmul(o_mla, w_mla_o, layer=l, epilogue="gate", extras=(z,), extra_col_blocks=(o_ga // tn_g,), tn=tn_g)
        y = matmul(o_nsa, w_nsa_o, layer=l, epilogue="gate_add", extras=(z, y1), extra_col_blocks=(o_gb // tn_g, 0),
                   out_dtype=BF16, tn=tn_g)
        x = matmul(y, w_out, layer=l, epilogue="add", extras=(x,))
        h2 = rms_norm_rows(x, g_mlp[l])
        u = matmul(h2, w_up, layer=l, epilogue="relu2", out_dtype=BF16)
        x = matmul(u, w_down, layer=l, epilogue="add", extras=(x,), tn=2 * MM_TN, tk=MM_TK // 2)

        pr = lambda a: a[:n_p].reshape(nb, t_p, -1)
        kw_p = pr(ta[:, dk:])
        vw_p = pr(z[:, o_td + dk:o_td + 2 * dk])
        p_rows.append((pr(c), pr(tb[:, :rope]), pr(z[:, o_tc:o_tc + dk]), pr(z[:, o_tc + dk:o_tc + 2 * dk]),
                       pr(ta[:, :dk]), pr(z[:, o_td:o_td + dk]), kw_p[:, t_p - wkeep_p:], vw_p[:, t_p - wkeep_p:]))
        s_rows.append((c_s, kr_s, kcr_s, vcr_s, ks_s, vs_s, kw_all[:, t_s:], vw_all[:, t_s:]))

    xp = x[:n_p].reshape(nb, t_p, d_model)
    xs = x[n_p:].reshape(db, t_s, d_model)
    (lat_p, krope_p, kcmp_p, vcmp_p, kslc_p, vslc_p, kwin_p, vwin_p) = [jnp.stack(a, axis=0) for a in zip(*p_rows)]
    (lat_s, krope_s, kcmp_s, vcmp_s, kslc_s, vslc_s, kwin_s, vwin_s) = [jnp.stack(a, axis=0) for a in zip(*s_rows)]
    return (xp, xs, lat_p, lat_s, krope_p, krope_s, kcmp_p, kcmp_s, vcmp_p, vcmp_s,
            kslc_p, kslc_s, vslc_p, vslc_s, kwin_p, kwin_s, vwin_p, vwin_s)
```

```python
import functools

import jax
import jax.numpy as jnp
import numpy as np
from jax import lax
from jax.experimental import pallas as pl
from jax.experimental.pallas import tpu as pltpu

CMP_STRIDE = 16
SLC_BLOCK = 64
N_SELECT = 16
WINDOW = 512
ROPE_THETA = 500000.0
EPS = 1e-6
NEG = -1e30
FORCE_SCORE = 1e9

F32 = jnp.float32
BF16 = jnp.bfloat16
LANES = 128
ATT_TQ = 128
KEY_SCALE_CHUNK = 256

VMEM_LIMIT_BYTES = 56 * 1024 * 1024

_NT = (((1,), (1,)), ((), ()))


def _cparams(*sem):
    return pltpu.CompilerParams(dimension_semantics=sem, vmem_limit_bytes=VMEM_LIMIT_BYTES)


def _pick_tile(dim, target, quantum):
    best = None
    t = quantum
    while t <= min(dim, target):
        if dim % t == 0:
            best = t
        t += quantum
    return best if best is not None else dim


def _mm_body(*refs, nk, epilogue, n_extra):
    a_ref, b_ref = refs[0], refs[1]
    extras = refs[2:2 + n_extra]
    o_ref = refs[2 + n_extra]
    acc_ref = refs[3 + n_extra] if nk > 1 else None

    part = jnp.dot(a_ref[...].astype(BF16), b_ref[...].astype(BF16), preferred_element_type=F32)

    def finish(acc):
        if epilogue == "none":
            r = acc
        elif epilogue == "relu2":
            r = jnp.square(jnp.maximum(acc, 0.0))
        elif epilogue == "add":
            r = acc + extras[0][...]
        elif epilogue == "gate":
            r = jax.nn.sigmoid(extras[0][...]) * acc
        elif epilogue == "gate_add":
            r = jax.nn.sigmoid(extras[0][...]) * acc + extras[1][...]
        else:
            raise ValueError(epilogue)
        o_ref[...] = r.astype(o_ref.dtype)

    if nk == 1:
        finish(part)
        return

    k = pl.program_id(2)

    @pl.when(k == 0)
    def _():
        acc_ref[...] = part

    @pl.when(k > 0)
    def _():
        acc_ref[...] += part

    @pl.when(k == nk - 1)
    def _():
        finish(acc_ref[...])


MM_TM, MM_TN, MM_TK = 1664, 256, 4096


def matmul(a, b, *, layer=None, epilogue="none", extras=(), extra_col_blocks=None, out_dtype=F32, tm=MM_TM, tn=MM_TN,
           tk=MM_TK):
    m, kdim = a.shape
    kdim2, n = b.shape[-2:]
    assert kdim == kdim2 and (b.ndim == 2) == (layer is None)
    tm = _pick_tile(m, tm, LANES)
    tk = _pick_tile(kdim, tk, LANES)
    tn = min(tn, n)
    nk = kdim // tk
    grid = (m // tm, pl.cdiv(n, tn), nk)
    if layer is None:
        b_spec = pl.BlockSpec((tk, tn), lambda i, j, k: (k, j))
    else:
        b_spec = pl.BlockSpec((None, tk, tn), lambda i, j, k: (layer, k, j))
    in_specs = [pl.BlockSpec((tm, tk), lambda i, j, k: (i, k)), b_spec]
    offs = extra_col_blocks or (0,) * len(extras)
    for o in offs:
        in_specs.append(pl.BlockSpec((tm, tn), lambda i, j, k, o=o: (i, j + o)))
    scratch = [pltpu.VMEM((tm, tn), F32)] if nk > 1 else []
    return pl.pallas_call(
        functools.partial(_mm_body, nk=nk, epilogue=epilogue, n_extra=len(extras)),
        grid=grid,
        in_specs=in_specs,
        out_specs=pl.BlockSpec((tm, tn), lambda i, j, k: (i, j)),
        out_shape=jax.ShapeDtypeStruct((m, n), out_dtype),
        scratch_shapes=scratch,
        compiler_params=_cparams("parallel", "parallel", "arbitrary"),
        name="mm_" + epilogue,
    )(a, b, *extras)


def _group_mean_sq(x, group):
    w = x.shape[1]
    xx = x * x
    hi = xx.astype(BF16)
    lo = (xx - hi.astype(F32)).astype(BF16)
    r = lax.broadcasted_iota(jnp.int32, (LANES, LANES), 0) // group
    c = lax.broadcasted_iota(jnp.int32, (LANES, LANES), 1) // group
    bd = jnp.where(r == c, 1.0, 0.0).astype(BF16)
    outs = []
    for j in range(w // LANES):
        sl = slice(j * LANES, (j + 1) * LANES)
        outs.append(jnp.dot(hi[:, sl], bd, preferred_element_type=F32)
                    + jnp.dot(lo[:, sl], bd, preferred_element_type=F32))
    ms = outs[0] if len(outs) == 1 else jnp.concatenate(outs, axis=1)
    return ms * (1.0 / group)


def _rope128(y, cos, sin, half, group):
    lane = lax.broadcasted_iota(jnp.int32, y.shape, 1) % group
    partner = jnp.where(lane < half, pltpu.roll(y, LANES - half, 1), pltpu.roll(y, half, 1))
    return y * cos + partner * sin


def _row_rms(x):
    return lax.rsqrt(jnp.mean(x * x, axis=-1, keepdims=True) + EPS)


def _p1_body(z_ref, gqa_ref, gkva_ref, gnq_ref, ga_ref, gb_ref, cn_ref, sn_ref, cm_ref, sm_ref,
             qa_ref, c_ref, nq_ref, ta_ref, tb_ref, *, q_rank, kv_rank, nqw, dk, nsa_half, mla_half, nsa_scale):
    o1 = q_rank + kv_rank
    x = z_ref[:, 0:q_rank]
    qa_ref[...] = (x * _row_rms(x) * gqa_ref[...]).astype(qa_ref.dtype)
    x = z_ref[:, q_rank:o1]
    c_ref[...] = x * _row_rms(x) * gkva_ref[...]
    cn, sn = cn_ref[...], sn_ref[...]
    heads_per_tile = LANES // dk
    for j in range(nqw // LANES):
        x = z_ref[:, o1 + j * LANES:o1 + (j + 1) * LANES]
        y = x * lax.rsqrt(_group_mean_sq(x, dk) + EPS) * gnq_ref[:, j * LANES:(j + 1) * LANES]
        y = _rope128(y, cn, sn, nsa_half, dk) * nsa_scale
        for hh in range(heads_per_tile):
            nq_ref[j * heads_per_tile + hh] = y[:, hh * dk:(hh + 1) * dk].astype(nq_ref.dtype)
    o2 = o1 + nqw
    x = z_ref[:, o2:o2 + LANES]
    y = x * lax.rsqrt(_group_mean_sq(x, dk) + EPS) * ga_ref[...]
    ta_ref[...] = _rope128(y, cn, sn, nsa_half, dk)
    x = z_ref[:, o2 + LANES:o2 + 2 * LANES]
    y = x * lax.rsqrt(_group_mean_sq(x, 2 * mla_half) + EPS) * gb_ref[...]
    tb_ref[...] = _rope128(y, cm_ref[...], sm_ref[...], mla_half, 2 * mla_half)


def post_in_proj(z, gqa, gkva, gnq_t, g_a, g_b, cn, sn, cm, sm, *, q_rank, kv_rank, nsa_heads, dk, mla_rope):
    m = z.shape[0]
    nqw = nsa_heads * dk
    win = q_rank + kv_rank + nqw + 2 * LANES
    tm = _pick_tile(m, 640, LANES)
    row = lambda i: (i, 0)
    cst = lambda i: (0, 0)
    return pl.pallas_call(
        functools.partial(_p1_body, q_rank=q_rank, kv_rank=kv_rank, nqw=nqw, dk=dk, nsa_half=dk // 8,
                          mla_half=mla_rope // 2, nsa_scale=dk ** -0.5),
        grid=(m // tm,),
        in_specs=[pl.BlockSpec((tm, win), row),
                  pl.BlockSpec((1, q_rank), cst), pl.BlockSpec((1, kv_rank), cst), pl.BlockSpec((1, nqw), cst),
                  pl.BlockSpec((1, LANES), cst), pl.BlockSpec((1, LANES), cst),
                  pl.BlockSpec((tm, LANES), row), pl.BlockSpec((tm, LANES), row),
                  pl.BlockSpec((tm, LANES), row), pl.BlockSpec((tm, LANES), row)],
        out_specs=[pl.BlockSpec((tm, q_rank), row), pl.BlockSpec((tm, kv_rank), row),
                   pl.BlockSpec((nsa_heads, tm, dk), lambda i: (0, i, 0)),
                   pl.BlockSpec((tm, LANES), row), pl.BlockSpec((tm, LANES), row)],
        out_shape=[jax.ShapeDtypeStruct((m, q_rank), BF16), jax.ShapeDtypeStruct((m, kv_rank), F32),
                   jax.ShapeDtypeStruct((nsa_heads, m, dk), BF16),
                   jax.ShapeDtypeStruct((m, LANES), F32), jax.ShapeDtypeStruct((m, LANES), F32)],
        compiler_params=_cparams("parallel"),
        name="post_in_proj",
    )(z, gqa, gkva, gnq_t, g_a, g_b, cn, sn, cm, sm)


def _p2_body(q_ref, gn_ref, gr_ref, wkb_ref, cm_ref, sm_ref, qabs_ref, qr_ref, *, heads, nope, rope, scale):
    for h in range(heads):
        x = q_ref[:, h * nope:(h + 1) * nope]
        y = x * lax.rsqrt(_group_mean_sq(x, nope) + EPS) * gn_ref[...] * scale
        qa = lax.dot_general(y.astype(BF16), wkb_ref[h].astype(BF16), _NT, preferred_element_type=F32)
        qabs_ref[h] = qa.astype(qabs_ref.dtype)
    o = heads * nope
    per_tile = LANES // rope
    cm, sm = cm_ref[...], sm_ref[...]
    for j in range(heads * rope // LANES):
        x = q_ref[:, o + j * LANES:o + (j + 1) * LANES]
        y = x * lax.rsqrt(_group_mean_sq(x, rope) + EPS) * gr_ref[...]
        y = _rope128(y, cm, sm, rope // 2, rope) * scale
        for hh in range(per_tile):
            qr_ref[j * per_tile + hh] = y[:, hh * rope:(hh + 1) * rope].astype(qr_ref.dtype)


def post_q_proj(q, gn, gr, w_kb, cm, sm, *, layer, heads, nope, rope, kv_rank, scale):
    m = q.shape[0]
    tm = _pick_tile(m, 640, LANES)
    row = lambda i: (i, 0)
    cst = lambda i: (0, 0)
    return pl.pallas_call(
        functools.partial(_p2_body, heads=heads, nope=nope, rope=rope, scale=scale),
        grid=(m // tm,),
        in_specs=[pl.BlockSpec((tm, heads * (nope + rope)), row),
                  pl.BlockSpec((1, nope), cst), pl.BlockSpec((1, LANES), cst),
                  pl.BlockSpec((None, heads, kv_rank, nope), lambda i: (layer, 0, 0, 0)),
                  pl.BlockSpec((tm, LANES), row), pl.BlockSpec((tm, LANES), row)],
        out_specs=[pl.BlockSpec((heads, tm, kv_rank), lambda i: (0, i, 0)),
                   pl.BlockSpec((heads, tm, rope), lambda i: (0, i, 0))],
        out_shape=[jax.ShapeDtypeStruct((heads, m, kv_rank), BF16), jax.ShapeDtypeStruct((heads, m, rope), BF16)],
        compiler_params=_cparams("parallel"),
        name="post_q_proj",
    )(q, gn, gr, w_kb, cm, sm)


def _key_scale(c_bf, wt_bf, heads, nope):
    n = c_bf.shape[0]
    step = min(n, KEY_SCALE_CHUNK)
    parts = []
    for t0 in range(0, n, step):
        kt = lax.dot_general(wt_bf, c_bf[t0:t0 + step], _NT, preferred_element_type=F32)
        kt = kt * kt
        parts.append(jnp.sum(kt.reshape(heads, nope, step), axis=1))
    ss = parts[0] if len(parts) == 1 else jnp.concatenate(parts, axis=1)
    return lax.rsqrt(ss * (1.0 / nope) + EPS)


def _ks_body(c_ref, wt_ref, r_ref, *, heads, nope):
    r_ref[...] = _key_scale(c_ref[...].astype(BF16), wt_ref[...], heads, nope)


def prompt_key_scale(c, wt_bf, *, nb, t, heads, nope, tk):
    kv_rank = c.shape[1]
    nch = t // tk
    return pl.pallas_call(
        functools.partial(_ks_body, heads=heads, nope=nope),
        grid=(nb, nch),
        in_specs=[pl.BlockSpec((tk, kv_rank), lambda b, j: (b * nch + j, 0)),
                  pl.BlockSpec((heads * nope, kv_rank), lambda b, j: (0, 0))],
        out_specs=pl.BlockSpec((None, None, heads, tk), lambda b, j: (b, j, 0, 0)),
        out_shape=jax.ShapeDtypeStruct((nb, nch, heads, tk), F32),
        compiler_params=_cparams("parallel", "parallel"),
        name="prompt_key_scale",
    )(c, wt_bf)


def _att_body(*refs, mode, heads, tq, tk, d1, d2, dv, k1_off, k2_off, v_off, has_r):
    it = iter(refs)
    q1_ref = next(it)
    q2_ref = next(it) if d2 else None
    k1_ref = next(it)
    k2_ref = next(it) if d2 else None
    v_ref = next(it)
    r_ref = next(it) if has_r else None
    sel_ref = next(it) if mode == "select" else None
    o_ref, m_scr, l_scr, acc_scr = next(it), next(it), next(it), next(it)

    i = pl.program_id(1)
    q0 = i * tq
    rows = heads * tq
    q1 = q1_ref[...].reshape(rows, d1)
    q2 = q2_ref[...].reshape(rows, d2) if d2 else None

    m_scr[...] = jnp.full(m_scr.shape, NEG, F32)
    l_scr[...] = jnp.zeros(l_scr.shape, F32)
    acc_scr[...] = jnp.zeros(acc_scr.shape, F32)

    hi = (q0 + tq + tk - 1) // tk

    def chunk(kci, carry):
        ks = pl.multiple_of(kci * tk, tk)
        k1c = k1_ref[pl.ds(ks, tk), k1_off:k1_off + d1].astype(BF16)
        s = lax.dot_general(q1, k1c, _NT, preferred_element_type=F32).reshape(heads, tq, tk)
        if has_r:
            s = s * r_ref[kci][:, None, :]
        if d2:
            k2c = k2_ref[pl.ds(ks, tk), k2_off:k2_off + d2].astype(BF16)
            s = s + lax.dot_general(q2, k2c, _NT, preferred_element_type=F32).reshape(heads, tq, tk)
        qpos = q0 + lax.broadcasted_iota(jnp.int32, (tq, tk), 0)
        kpos = ks + lax.broadcasted_iota(jnp.int32, (tq, tk), 1)
        dist = qpos - kpos
        if mode == "select":
            blk = (ks + lax.broadcasted_iota(jnp.int32, (LANES, tk), 1)) // SLC_BLOCK
            e = jnp.where(blk == lax.broadcasted_iota(jnp.int32, (LANES, tk), 0), 1.0, 0.0).astype(BF16)
            selm = jnp.dot(sel_ref[...], e, preferred_element_type=F32)
            mask = jnp.where(dist >= 0, selm, 0.0) > 0.5
        else:
            mask = dist >= 0
        mask = mask[None]
        s = jnp.where(mask, s, NEG)
        m_prev = m_scr[...]
        m_new = jnp.maximum(m_prev, jnp.max(s, axis=-1, keepdims=True))
        alpha = jnp.exp(m_prev - m_new)
        p = jnp.exp(s - m_new)
        l_scr[...] = alpha * l_scr[...] + jnp.sum(p, axis=-1, keepdims=True)
        vc = v_ref[pl.ds(ks, tk), v_off:v_off + dv].astype(BF16)
        pv = jnp.dot(p.reshape(rows, tk).astype(BF16), vc, preferred_element_type=F32)
        acc_scr[...] = alpha * acc_scr[...] + pv.reshape(heads, tq, dv)
        m_scr[...] = m_new
        return carry

    lax.fori_loop(0, hi, chunk, 0)
    o_ref[...] = (acc_scr[...] / l_scr[...]).astype(o_ref.dtype)


def prompt_attention(q1, k1, v, *, mode, nb, t, tk, d1, dv, k1_off=0, v_off=0, q2=None, k2=None, d2=0, k2_off=0,
                     r=None, sel=None, v_col_block=0):
    heads = q1.shape[0]
    tq = ATT_TQ
    nq = t // tq
    qmap = lambda b, i: (0, b * nq + i, 0)
    kmap = lambda b, i: (b, 0)
    args, specs = [q1], [pl.BlockSpec((heads, tq, d1), qmap)]
    if d2:
        args.append(q2)
        specs.append(pl.BlockSpec((heads, tq, d2), qmap))
    args.append(k1)
    specs.append(pl.BlockSpec((t, k1.shape[1]), kmap))
    if d2:
        args.append(k2)
        specs.append(pl.BlockSpec((t, k2.shape[1]), kmap))
    args.append(v)
    vw = LANES if v.shape[1] > max(LANES, dv) else v.shape[1]
    specs.append(pl.BlockSpec((t, vw), lambda b, i: (b, v_col_block)))
    if r is not None:
        args.append(r)
        specs.append(pl.BlockSpec((None, t // tk, heads, tk), lambda b, i: (b, 0, 0, 0)))
    if sel is not None:
        args.append(sel)
        specs.append(pl.BlockSpec((tq, LANES), lambda b, i: (b * nq + i, 0)))
    return pl.pallas_call(
        functools.partial(_att_body, mode=mode, heads=heads, tq=tq, tk=tk, d1=d1, d2=d2, dv=dv,
                          k1_off=k1_off, k2_off=k2_off, v_off=v_off, has_r=r is not None),
        grid=(nb, nq),
        in_specs=specs,
        out_specs=pl.BlockSpec((heads, tq, dv), qmap),
        out_shape=jax.ShapeDtypeStruct((heads, q1.shape[1], dv), BF16),
        scratch_shapes=[pltpu.VMEM((heads, tq, 1), F32), pltpu.VMEM((heads, tq, 1), F32),
                        pltpu.VMEM((heads, tq, dv), F32)],
        compiler_params=_cparams("parallel", "parallel"),
        name="prompt_att_" + mode,
    )(*args)


def _win_body(q_ref, k_ref, v_ref, o_ref, *, heads, tq, wk, d, k_off, v_off):
    q0 = pl.program_id(1) * tq
    rows = heads * tq
    q = q_ref[...].reshape(rows, d)
    start = pl.multiple_of(jnp.maximum(q0 + tq - wk, 0), tq)
    kc = k_ref[pl.ds(start, wk), k_off:k_off + d].astype(BF16)
    s = lax.dot_general(q, kc, _NT, preferred_element_type=F32).reshape(heads, tq, wk)
    dist = (q0 - start) + lax.broadcasted_iota(jnp.int32, (tq, wk), 0) - lax.broadcasted_iota(jnp.int32, (tq, wk), 1)
    mask = ((dist >= 0) & (dist < WINDOW))[None]
    s = jnp.where(mask, s, NEG)
    p = jnp.exp(s - jnp.max(s, axis=-1, keepdims=True))
    l = jnp.sum(p, axis=-1, keepdims=True)
    vc = v_ref[pl.ds(start, wk), v_off:v_off + d].astype(BF16)
    o = jnp.dot(p.reshape(rows, wk).astype(BF16), vc, preferred_element_type=F32).reshape(heads, tq, d)
    o_ref[...] = (o / l).astype(o_ref.dtype)


def prompt_window_attention(q, k, v, *, nb, t, d, k_off, v_off, v_col_block):
    heads = q.shape[0]
    tq = ATT_TQ
    nq = t // tq
    wk = min(WINDOW + tq, t)
    assert wk % tq == 0
    qmap = lambda b, i: (0, b * nq + i, 0)
    return pl.pallas_call(
        functools.partial(_win_body, heads=heads, tq=tq, wk=wk, d=d, k_off=k_off, v_off=v_off),
        grid=(nb, nq),
        in_specs=[pl.BlockSpec((heads, tq, d), qmap), pl.BlockSpec((t, k.shape[1]), lambda b, i: (b, 0)),
                  pl.BlockSpec((t, LANES), lambda b, i: (b, v_col_block))],
        out_specs=pl.BlockSpec((heads, tq, d), qmap),
        out_shape=jax.ShapeDtypeStruct((heads, q.shape[1], d), BF16),
        compiler_params=_cparams("parallel", "parallel"),
        name="prompt_att_window",
    )(q, k, v)


def _compress_core(rows_ref, nblk, ptop_ref, pbot_ref, wtop_ref, wbot_ref, b1_ref, w2_ref, g_ref, cos_ref, sin_ref,
                   *, dk):
    x = jnp.concatenate([rows_ref[pl.ds(j, nblk, stride=CMP_STRIDE), :] for j in range(CMP_STRIDE)], axis=1)
    top = jnp.dot((x + ptop_ref[...]).astype(BF16), wtop_ref[...], preferred_element_type=F32)
    bot = jnp.dot((x + pbot_ref[...]).astype(BF16), wbot_ref[...], preferred_element_type=F32)
    h = top + pltpu.roll(bot, nblk - 1, 0) + b1_ref[...]
    kv = jnp.dot(jax.nn.gelu(h).astype(BF16), w2_ref[...], preferred_element_type=F32)
    lane = lax.broadcasted_iota(jnp.int32, kv.shape, 1)
    y = jnp.where(lane < dk, kv * lax.rsqrt(_group_mean_sq(kv, dk) + EPS) * g_ref[...], kv)
    return _rope128(y, cos_ref[...], sin_ref[...], dk // 8, dk)


def pack_compress_weights(cmp_pos_l, cmp_w1_l, cmp_b1_l, cmp_w2_l, g_nkc_l):
    _, cmp_len, dk = cmp_pos_l.shape
    hid = cmp_w1_l.shape[2]
    half = cmp_len // 2
    pos = jnp.concatenate([cmp_pos_l[0], cmp_pos_l[1]], axis=1)
    w1k = cmp_w1_l[0].reshape(cmp_len, dk, hid)
    w1v = cmp_w1_l[1].reshape(cmp_len, dk, hid)
    zeros = jnp.zeros_like(w1k)
    w1 = jnp.concatenate([jnp.concatenate([w1k, zeros], axis=2), jnp.concatenate([zeros, w1v], axis=2)], axis=1)
    zk = jnp.zeros((hid, dk), F32)
    w2 = jnp.concatenate([jnp.concatenate([cmp_w2_l[0], zk], axis=1), jnp.concatenate([zk, cmp_w2_l[1]], axis=1)], axis=0)
    return (pos[:half].reshape(1, -1), pos[half:].reshape(1, -1),
            w1[:half].reshape(half * 2 * dk, 2 * hid).astype(BF16), w1[half:].reshape(half * 2 * dk, 2 * hid).astype(BF16),
            cmp_b1_l.reshape(1, 2 * hid), w2.astype(BF16),
            jnp.concatenate([g_nkc_l, jnp.ones((LANES - dk,), F32)]).reshape(1, LANES))


def _cmp_prompt_body(rows_ref, ptop_ref, pbot_ref, wtop_ref, wbot_ref, b1_ref, w2_ref, g_ref, cos_ref, sin_ref, o_ref,
                     *, nblk, dk):
    o_ref[...] = _compress_core(rows_ref, nblk, ptop_ref, pbot_ref, wtop_ref, wbot_ref, b1_ref, w2_ref, g_ref,
                                cos_ref, sin_ref, dk=dk)


def _cmp_weight_specs(dk, hid, nblk):
    z2 = (lambda *a: (0, 0))
    wide = CMP_STRIDE * 2 * dk
    return [pl.BlockSpec((1, wide), z2), pl.BlockSpec((1, wide), z2),
            pl.BlockSpec((wide, 2 * hid), z2), pl.BlockSpec((wide, 2 * hid), z2),
            pl.BlockSpec((1, 2 * hid), z2), pl.BlockSpec((2 * hid, LANES), z2), pl.BlockSpec((1, LANES), z2),
            pl.BlockSpec((nblk, LANES), z2), pl.BlockSpec((nblk, LANES), z2)]


def compress_prompt(z, cmpw, *, nb, t, col_block, dk, hid):
    nblk = t // CMP_STRIDE
    return pl.pallas_call(
        functools.partial(_cmp_prompt_body, nblk=nblk, dk=dk),
        grid=(nb,),
        in_specs=[pl.BlockSpec((t, LANES), lambda b: (b, col_block))] + _cmp_weight_specs(dk, hid, nblk),
        out_specs=pl.BlockSpec((None, nblk, LANES), lambda b: (b, 0, 0)),
        out_shape=jax.ShapeDtypeStruct((nb, nblk, LANES), F32),
        compiler_params=_cparams("parallel"),
        name="compress_prompt",
    )(z, *cmpw)


def _select_rank(imp, qpos, ns):
    lane = lax.broadcasted_iota(jnp.int32, imp.shape, 1)
    cur = qpos // SLC_BLOCK
    forced = (lane == 0) | (lane == cur) | (lane == cur - 1)
    valid = lane * SLC_BLOCK <= qpos
    sc = jnp.where(forced, FORCE_SCORE, jnp.where(valid, imp, -FORCE_SCORE))
    sc = jnp.where(lane < ns, sc, -2.0 * FORCE_SCORE)
    rank = jnp.zeros(imp.shape, F32)
    for s in range(ns):
        col = sc[:, s:s + 1]
        beats = (col > sc) | ((col == sc) & (lane > s))
        rank = rank + jnp.where(beats, 1.0, 0.0)
    return rank, lane


def _select_mask(imp, qpos, ns):
    rank, lane = _select_rank(imp, qpos, ns)
    return jnp.where((rank < float(min(N_SELECT, ns))) & (lane < ns), 1.0, 0.0)


def _select_indices(imp, qpos, ns):
    rank, lane = _select_rank(imp, qpos, ns)
    lanef = lane.astype(F32)
    out_lane = lax.broadcasted_iota(jnp.int32, (imp.shape[0], LANES), 1)
    out = jnp.zeros((imp.shape[0], LANES), F32)
    for j in range(N_SELECT):
        idx = jnp.sum(jnp.where((rank == float(j)) & (lane < ns), lanef, 0.0), axis=-1, keepdims=True)
        out = jnp.where(out_lane == j, idx, out)
    return out.astype(jnp.int32)


def _overlap_matrix(ncp, cmp_len, width=LANES):
    n = lax.broadcasted_iota(jnp.int32, (ncp, width), 0)
    s = lax.broadcasted_iota(jnp.int32, (ncp, width), 1)
    cstart = n * CMP_STRIDE
    cend = cstart + cmp_len - 1
    bstart = s * SLC_BLOCK
    return jnp.where((cstart < bstart + SLC_BLOCK) & (cend >= bstart), 1.0, 0.0).astype(BF16)


def _cmpatt_body(q_ref, kv_ref, o_ref, sel_ref, *, heads, tq, dk, nc, cmp_len, ns):
    i = pl.program_id(1)
    rows = heads * tq
    ncp = kv_ref.shape[0]
    q = q_ref[...].reshape(rows, dk)
    kv = kv_ref[...]
    kc = kv[:, :dk].astype(BF16)
    vc = kv[:, dk:2 * dk].astype(BF16)
    s = lax.dot_general(q, kc, _NT, preferred_element_type=F32).reshape(heads, tq, ncp)
    qpos = i * tq + lax.broadcasted_iota(jnp.int32, (tq, ncp), 0)
    n = lax.broadcasted_iota(jnp.int32, (tq, ncp), 1)
    mask = ((n * CMP_STRIDE + cmp_len - 1 <= qpos) & (n < nc))[None]
    s = jnp.where(mask, s, NEG)
    m = jnp.max(s, axis=-1, keepdims=True)
    e = jnp.where(mask, jnp.exp(s - m), 0.0)
    p = e / jnp.maximum(jnp.sum(e, axis=-1, keepdims=True), 1e-30)
    o = jnp.dot(p.reshape(rows, ncp).astype(BF16), vc, preferred_element_type=F32)
    o_ref[...] = o.reshape(heads, tq, dk).astype(o_ref.dtype)
    psum = jnp.sum(p, axis=0)
    hi = psum.astype(BF16)
    lo = (psum - hi.astype(F32)).astype(BF16)
    ov = _overlap_matrix(ncp, cmp_len)
    imp = jnp.dot(hi, ov, preferred_element_type=F32) + jnp.dot(lo, ov, preferred_element_type=F32)
    qp = i * tq + lax.broadcasted_iota(jnp.int32, (tq, 1), 0)
    sel_ref[...] = _select_mask(imp, qp, ns).astype(sel_ref.dtype)


def prompt_cmp_attention(nq, kcvc, *, nb, t, dk, cmp_len):
    heads, m, _ = nq.shape
    tq = ATT_TQ
    nqb = t // tq
    ncp = kcvc.shape[1]
    nc = (t - cmp_len) // CMP_STRIDE + 1
    ns = -(-t // SLC_BLOCK)
    assert ns <= LANES and ncp <= LANES
    qmap = lambda b, i: (0, b * nqb + i, 0)
    return pl.pallas_call(
        functools.partial(_cmpatt_body, heads=heads, tq=tq, dk=dk, nc=nc, cmp_len=cmp_len, ns=ns),
        grid=(nb, nqb),
        in_specs=[pl.BlockSpec((heads, tq, dk), qmap), pl.BlockSpec((None, ncp, LANES), lambda b, i: (b, 0, 0))],
        out_specs=[pl.BlockSpec((heads, tq, dk), qmap), pl.BlockSpec((tq, LANES), lambda b, i: (b * nqb + i, 0))],
        out_shape=[jax.ShapeDtypeStruct((heads, m, dk), BF16), jax.ShapeDtypeStruct((m, LANES), BF16)],
        compiler_params=_cparams("parallel", "parallel"),
        name="prompt_cmp_att",
    )(nq, kcvc)


def _gate_body(oc_ref, os_ref, ow_ref, g_ref, e_ref, o_ref, *, heads, dk):
    sig = jax.nn.sigmoid(g_ref[...])
    hi = sig.astype(BF16)
    lo = (sig - hi.astype(F32)).astype(BF16)
    gates = [jnp.dot(hi, e_ref[i], preferred_element_type=F32) + jnp.dot(lo, e_ref[i], preferred_element_type=F32)
             for i in range(3)]
    for h in range(heads):
        sl = slice(h * dk, (h + 1) * dk)
        o = (gates[0][:, sl] * oc_ref[h].astype(F32) + gates[1][:, sl] * os_ref[h].astype(F32)
             + gates[2][:, sl] * ow_ref[h].astype(F32))
        o_ref[:, sl] = o.astype(o_ref.dtype)


def nsa_gate_combine(o_c, o_s, o_w, z, expand, *, gate_col_block):
    heads, m, dk = o_c.shape
    tm = _pick_tile(m, 640, LANES)
    hmap = lambda i: (0, i, 0)
    return pl.pallas_call(
        functools.partial(_gate_body, heads=heads, dk=dk),
        grid=(m // tm,),
        in_specs=[pl.BlockSpec((heads, tm, dk), hmap)] * 3
        + [pl.BlockSpec((tm, LANES), lambda i: (i, gate_col_block)),
           pl.BlockSpec((3, LANES, heads * dk), lambda i: (0, 0, 0))],
        out_specs=pl.BlockSpec((tm, heads * dk), lambda i: (i, 0)),
        out_shape=jax.ShapeDtypeStruct((m, heads * dk), BF16),
        compiler_params=_cparams("parallel"),
        name="nsa_gate_combine",
    )(o_c, o_s, o_w, z, expand)


def _headmm_body(x_ref, w_ref, o_ref):
    o_ref[...] = jnp.dot(x_ref[...], w_ref[...].astype(BF16), preferred_element_type=F32).astype(o_ref.dtype)


def head_matmul(x, w, *, layer):
    heads, m, k = x.shape
    n = w.shape[3]
    tm = _pick_tile(m, 1664, LANES)
    return pl.pallas_call(
        _headmm_body,
        grid=(m // tm, heads),
        in_specs=[pl.BlockSpec((None, tm, k), lambda i, h: (h, i, 0)),
                  pl.BlockSpec((None, None, k, n), lambda i, h: (layer, h, 0, 0))],
        out_specs=pl.BlockSpec((tm, n), lambda i, h: (i, h)),
        out_shape=jax.ShapeDtypeStruct((m, heads * n), BF16),
        compiler_params=_cparams("parallel", "parallel"),
        name="head_matmul",
    )(x, w)


def _rms_body(x_ref, g_ref, o_ref):
    x = x_ref[...]
    o_ref[...] = (x * _row_rms(x) * g_ref[...]).astype(o_ref.dtype)


def rms_norm_rows(x, g):
    m, d = x.shape
    tm = _pick_tile(m, 640, LANES)
    return pl.pallas_call(
        _rms_body,
        grid=(m // tm,),
        in_specs=[pl.BlockSpec((tm, d), lambda i: (i, 0)), pl.BlockSpec((1, d), lambda i: (0, 0))],
        out_specs=pl.BlockSpec((tm, d), lambda i: (i, 0)),
        out_shape=jax.ShapeDtypeStruct((m, d), BF16),
        compiler_params=_cparams("parallel"),
        name="rms_norm_rows",
    )(x, g.reshape(1, d))


PAGES_PER_STEP = 16


def _page_specs(layer, tile, n_pages, gp):
    def mk(i):
        return pl.BlockSpec((None, None) + tile, lambda b, g, pt: (layer, pt[b * n_pages + g * gp + i], 0, 0))
    return [mk(i) for i in range(gp)]


def _cat_pages(refs, axis=0):
    return jnp.concatenate([r[...] for r in refs], axis=axis) if len(refs) > 1 else refs[0][...]


def _token_minor(pool):
    return jnp.swapaxes(pool, 2, 3)


def _online_update(m_scr, l_scr, acc_scr, s, v_bf, mask=None):
    if mask is not None:
        s = jnp.where(mask, s, NEG)
    m_prev = m_scr[...]
    m_new = jnp.maximum(m_prev, jnp.max(s, axis=-1, keepdims=True))
    alpha = jnp.exp(m_prev - m_new)
    p = jnp.exp(s - m_new)
    if mask is not None:
        p = jnp.where(mask, p, 0.0)
    l_scr[...] = alpha * l_scr[...] + jnp.sum(p, axis=-1, keepdims=True)
    acc_scr[...] = alpha * acc_scr[...] + jnp.dot(p.astype(BF16), v_bf, preferred_element_type=F32)
    m_scr[...] = m_new


def _smla_body(pt_ref, *refs, gp, heads, nope, rope):
    del pt_ref
    lat = refs[:gp]
    krp = refs[gp:2 * gp]
    qa_ref, qr_ref, cs_ref, krs_ref, wt_ref, o_ref, m_scr, l_scr, acc_scr = refs[2 * gp:]
    g = pl.program_id(1)

    @pl.when(g == 0)
    def _():
        m_scr[...] = jnp.full(m_scr.shape, NEG, F32)
        l_scr[...] = jnp.zeros(l_scr.shape, F32)
        acc_scr[...] = jnp.zeros(acc_scr.shape, F32)

    qa, qr, wt = qa_ref[...], qr_ref[...], wt_ref[...]

    wq = jnp.concatenate([wt, qa], axis=0)

    def attend(c_bf, s_rope, mask):
        n = c_bf.shape[0]
        step = min(n, KEY_SCALE_CHUNK)
        rs, ss = [], []
        for t0 in range(0, n, step):
            kt = lax.dot_general(wq, c_bf[t0:t0 + step], _NT, preferred_element_type=F32)
            ss.append(kt[heads * nope:])
            kk = kt[:heads * nope]
            kk = kk * kk
            rs.append(jnp.sum(kk.reshape(heads, nope, step), axis=1))
        r = lax.rsqrt((rs[0] if len(rs) == 1 else jnp.concatenate(rs, axis=1)) * (1.0 / nope) + EPS)
        s_raw = ss[0] if len(ss) == 1 else jnp.concatenate(ss, axis=1)
        s = s_raw * r + s_rope
        _online_update(m_scr, l_scr, acc_scr, s, c_bf, mask)

    kr_t = _cat_pages(krp, axis=1).astype(BF16)
    attend(_cat_pages(lat).astype(BF16), jnp.dot(qr, kr_t, preferred_element_type=F32), None)

    @pl.when(g == pl.num_programs(1) - 1)
    def _():
        c_new = jnp.broadcast_to(cs_ref[...], (LANES, cs_ref.shape[1])).astype(BF16)
        s_rope = jnp.sum(qr.astype(F32) * krs_ref[...][:, :rope], axis=-1, keepdims=True)
        attend(c_new, s_rope, lax.broadcasted_iota(jnp.int32, (heads, LANES), 1) == 0)
        o_ref[...] = (acc_scr[...] / l_scr[...]).astype(o_ref.dtype)


def sample_mla(page_table_flat, lat_pool, krope_pool_t, qabs_s, qrope_s, c_new, tb_new, wt_bf, *, layer, heads, nope):
    db, _, kv_rank = qabs_s.shape
    rope = qrope_s.shape[2]
    page_size = lat_pool.shape[2]
    n_pages = page_table_flat.shape[0] // db
    gp = min(PAGES_PER_STEP, n_pages)
    assert n_pages % gp == 0
    per_b = lambda w1, w2: pl.BlockSpec((None, w1, w2), lambda b, g, pt: (b, 0, 0))
    grid_spec = pltpu.PrefetchScalarGridSpec(
        num_scalar_prefetch=1,
        grid=(db, n_pages // gp),
        in_specs=_page_specs(layer, (page_size, kv_rank), n_pages, gp) + _page_specs(layer, (rope, page_size), n_pages, gp)
        + [per_b(heads, kv_rank), per_b(heads, rope), per_b(1, kv_rank), per_b(1, tb_new.shape[2]),
           pl.BlockSpec((heads * nope, kv_rank), lambda b, g, pt: (0, 0))],
        out_specs=per_b(heads, kv_rank),
        scratch_shapes=[pltpu.VMEM((heads, 1), F32), pltpu.VMEM((heads, 1), F32), pltpu.VMEM((heads, kv_rank), F32)],
    )
    return pl.pallas_call(
        functools.partial(_smla_body, gp=gp, heads=heads, nope=nope, rope=rope),
        grid_spec=grid_spec,
        out_shape=jax.ShapeDtypeStruct((db, heads, kv_rank), BF16),
        compiler_params=_cparams("parallel", "arbitrary"),
        name="sample_mla",
    )(page_table_flat, *([lat_pool] * gp), *([krope_pool_t] * gp), qabs_s, qrope_s, c_new, tb_new, wt_bf)


def _scmp_body(pt_ref, *refs, gp, heads, dk, page_size, nblk, nc, cmp_len, ns, nsp, past_len):
    del pt_ref
    kpg = refs[:gp]
    vpg = refs[gp:2 * gp]
    (q_ref, ptop_ref, pbot_ref, wtop_ref, wbot_ref, b1_ref, w2_ref, gk_ref, cos_ref, sin_ref,
     oc_ref, sel_ref, rows_scr) = refs[2 * gp:]
    g = pl.program_id(1)
    for i in range(gp):
        row0 = pl.multiple_of((g * gp + i) * page_size, page_size)
        rows_scr[pl.ds(row0, page_size), :] = jnp.concatenate([kpg[i][...], vpg[i][...]], axis=0).T

    @pl.when(g == pl.num_programs(1) - 1)
    def _():
        kv = _compress_core(rows_scr, nblk, ptop_ref, pbot_ref, wtop_ref, wbot_ref, b1_ref, w2_ref, gk_ref,
                            cos_ref, sin_ref, dk=dk)
        kc = kv[:, :dk].astype(BF16)
        vc = kv[:, dk:2 * dk].astype(BF16)
        s = lax.dot_general(q_ref[...], kc, _NT, preferred_element_type=F32)
        n = lax.broadcasted_iota(jnp.int32, s.shape, 1)
        mask = (n * CMP_STRIDE + cmp_len - 1 <= past_len) & (n < nc)
        s = jnp.where(mask, s, NEG)
        mx = jnp.max(s, axis=-1, keepdims=True)
        e = jnp.where(mask, jnp.exp(s - mx), 0.0)
        p = e / jnp.maximum(jnp.sum(e, axis=-1, keepdims=True), 1e-30)
        oc_ref[...] = jnp.dot(p.astype(BF16), vc, preferred_element_type=F32)
        psum = jnp.broadcast_to(jnp.sum(p, axis=0, keepdims=True), (8, nblk))
        hi = psum.astype(BF16)
        lo = (psum - hi.astype(F32)).astype(BF16)
        ov = _overlap_matrix(nblk, cmp_len, nsp)
        imp = jnp.dot(hi, ov, preferred_element_type=F32) + jnp.dot(lo, ov, preferred_element_type=F32)
        sel = _select_indices(imp, jnp.full((8, 1), past_len, jnp.int32), ns)
        sel_ref[...] = sel[0:1, :]


def sample_compress_attend(page_table_flat, kpool, vpool, nq_s, cmpw, *, layer, cmp_len, hid, past_len):
    db, heads, dk = nq_s.shape
    page_size = kpool.shape[3]
    n_pages = page_table_flat.shape[0] // db
    gp = min(PAGES_PER_STEP, n_pages)
    nblk = past_len // CMP_STRIDE
    nc = (past_len + 1 - cmp_len) // CMP_STRIDE + 1
    ns = -(-(past_len + 1) // SLC_BLOCK)
    nsp = -(-ns // LANES) * LANES
    assert n_pages % gp == 0 and nc <= nblk and ns >= N_SELECT and 2 * dk == LANES
    grid_spec = pltpu.PrefetchScalarGridSpec(
        num_scalar_prefetch=1,
        grid=(db, n_pages // gp),
        in_specs=_page_specs(layer, (dk, page_size), n_pages, gp) + _page_specs(layer, (dk, page_size), n_pages, gp)
        + [pl.BlockSpec((None, heads, dk), lambda b, g, pt: (b, 0, 0))] + _cmp_weight_specs(dk, hid, nblk),
        out_specs=[pl.BlockSpec((None, heads, dk), lambda b, g, pt: (b, 0, 0)),
                   pl.BlockSpec((None, 1, LANES), lambda b, g, pt: (b, 0, 0))],
        scratch_shapes=[pltpu.VMEM((past_len, LANES), F32)],
    )
    return pl.pallas_call(
        functools.partial(_scmp_body, gp=gp, heads=heads, dk=dk, page_size=page_size, nblk=nblk, nc=nc,
                          cmp_len=cmp_len, ns=ns, nsp=nsp, past_len=past_len),
        grid_spec=grid_spec,
        out_shape=[jax.ShapeDtypeStruct((db, heads, dk), F32), jax.ShapeDtypeStruct((db, 1, LANES), jnp.int32)],
        compiler_params=_cparams("parallel", "arbitrary"),
        name="sample_compress_attend",
    )(page_table_flat, *([kpool] * gp), *([vpool] * gp), nq_s, *cmpw)


def _sel_page(si, b, j, bpp, n_pages):
    return jnp.clip(si[b * LANES + j] // bpp, 0, n_pages - 1)


def _ssel_body(pt_ref, si_ref, *refs, nsel, heads, dk, page_size, n_pages, w0):
    del pt_ref
    kpg = refs[:nsel]
    vpg = refs[nsel:2 * nsel]
    q_ref, ta_ref, td_ref, kwin_ref, vwin_ref, oc_ref, ng_ref, o_ref = refs[2 * nsel:]
    b = pl.program_id(0)
    bpp = page_size // SLC_BLOCK
    q = q_ref[...]
    qf = q.astype(F32)
    ta = ta_ref[...]
    td = td_ref[...]

    lane = lax.broadcasted_iota(jnp.int32, (1, page_size), 1)
    pieces = []
    for j in range(nsel):
        blk = si_ref[b * LANES + j]
        blk = jnp.where(blk < n_pages * bpp, blk, -1)
        tok_blk = (_sel_page(si_ref, b, j, bpp, n_pages) * page_size + lane) // SLC_BLOCK
        pieces.append(jnp.where(tok_blk == blk, 1.0, 0.0))
    mask = jnp.concatenate(pieces, axis=1) > 0.5
    s = jnp.dot(q, _cat_pages(kpg, axis=1).astype(BF16), preferred_element_type=F32)
    mask = jnp.broadcast_to(mask, s.shape)
    s = jnp.where(mask, s, NEG)
    s_new = jnp.sum(qf * ta[:, :dk], axis=-1, keepdims=True)
    ms = jnp.maximum(jnp.max(s, axis=-1, keepdims=True), s_new)
    p = jnp.where(mask, jnp.exp(s - ms), 0.0)
    p_new = jnp.exp(s_new - ms)
    pv = lax.dot_general(p.astype(BF16), _cat_pages(vpg, axis=1).astype(BF16), _NT, preferred_element_type=F32)
    o_s = (pv + p_new * td[:, :dk]) / (jnp.sum(p, axis=-1, keepdims=True) + p_new)

    sw = jnp.dot(q, kwin_ref[...].astype(BF16), preferred_element_type=F32)
    wmask = lax.broadcasted_iota(jnp.int32, sw.shape, 1) >= w0
    sw = jnp.where(wmask, sw, NEG)
    sw_new = jnp.sum(qf * ta[:, dk:2 * dk], axis=-1, keepdims=True)
    mw = jnp.maximum(jnp.max(sw, axis=-1, keepdims=True), sw_new)
    pw = jnp.where(wmask, jnp.exp(sw - mw), 0.0)
    pw_new = jnp.exp(sw_new - mw)
    o_w = ((lax.dot_general(pw.astype(BF16), vwin_ref[...].astype(BF16), _NT, preferred_element_type=F32)
            + pw_new * td[:, dk:2 * dk]) / (jnp.sum(pw, axis=-1, keepdims=True) + pw_new))
    gate = jax.nn.sigmoid(ng_ref[...])
    o = gate[:, 0:1] * oc_ref[...] + gate[:, 1:2] * o_s + gate[:, 2:3] * o_w
    o_ref[...] = o.astype(o_ref.dtype)


def sample_select_window(page_table_flat, sel_flat, kpool, vpool, kwin, vwin, nq_s, ta_new, td_new, o_c, ngate_s, *,
                         layer):
    db, heads, dk = nq_s.shape
    page_size = kpool.shape[3]
    n_pages = page_table_flat.shape[0] // db
    bpp = page_size // SLC_BLOCK
    wbuf = kwin.shape[3]
    w0 = max(0, wbuf + 1 - WINDOW)
    per_b = lambda w1, w2: pl.BlockSpec((None, w1, w2), lambda b, pt, si: (b, 0, 0))
    win = pl.BlockSpec((None, None, dk, wbuf), lambda b, pt, si: (layer, b, 0, 0))

    def sel_spec(j):
        return pl.BlockSpec((None, None, dk, page_size),
                            lambda b, pt, si: (layer, pt[b * n_pages + _sel_page(si, b, j, bpp, n_pages)], 0, 0))
    sel_specs = [sel_spec(j) for j in range(N_SELECT)]
    grid_spec = pltpu.PrefetchScalarGridSpec(
        num_scalar_prefetch=2,
        grid=(db,),
        in_specs=sel_specs + sel_specs
        + [per_b(heads, dk), per_b(1, LANES), per_b(1, LANES), win, win, per_b(heads, dk), per_b(heads, 3)],
        out_specs=per_b(heads, dk),
    )
    return pl.pallas_call(
        functools.partial(_ssel_body, nsel=N_SELECT, heads=heads, dk=dk, page_size=page_size, n_pages=n_pages, w0=w0),
        grid_spec=grid_spec,
        out_shape=jax.ShapeDtypeStruct((db, heads, dk), BF16),
        compiler_params=_cparams("parallel"),
        name="sample_select_window",
    )(page_table_flat, sel_flat, *([kpool] * N_SELECT), *([vpool] * N_SELECT), nq_s, ta_new, td_new, kwin, vwin, o_c,
      ngate_s)


def _rope_tables(pos, group, rot_dim):
    half = rot_dim // 2
    inv = ROPE_THETA ** (-jnp.arange(half, dtype=F32) / half)
    ang = pos.astype(F32)[:, None] * inv[None, :]
    cos, sin = jnp.cos(ang), jnp.sin(ang)
    ones = jnp.ones((pos.shape[0], group - rot_dim), F32)
    c = jnp.concatenate([cos, cos, ones], axis=1)
    s = jnp.concatenate([-sin, sin, 0.0 * ones], axis=1)
    reps = LANES // group
    return jnp.tile(c, (1, reps)), jnp.tile(s, (1, reps))


def kernel(x_prompt, x_sample, cache_mla_latent, cache_mla_krope, cache_nsa_kcmp, cache_nsa_vcmp,
           cache_nsa_kslc, cache_nsa_vslc, state_nsa_kwin, state_nsa_vwin, page_table,
           g_attn, w_in, g_qa, w_qb, g_kva, g_qn, g_qr, g_kn, g_kr, w_kb, w_vb, w_mla_o,
           g_nq, g_nkc, g_nks, g_nkw, cmp_pos, cmp_w1, cmp_b1, cmp_w2, w_nsa_o,
           w_out, g_mlp, w_up, w_down):
    nb, t_p, d_model = x_prompt.shape
    db, t_s, _ = x_sample.shape
    assert t_s == 1
    depth = w_in.shape[0]
    q_rank = g_qa.shape[1]
    kv_rank = g_kva.shape[1]
    nope = g_qn.shape[1]
    rope = g_qr.shape[1]
    heads = w_kb.shape[1]
    mla_v = w_vb.shape[3]
    dk = g_nq.shape[1]
    nsa_heads = w_nsa_o.shape[1] // dk
    cmp_len = cmp_pos.shape[2]
    hid = cmp_w1.shape[3]
    page_size = cache_mla_latent.shape[2]
    past_len = page_table.shape[1] * page_size
    wkeep_p = min(WINDOW, t_p)
    mla_scale = (nope + rope) ** -0.5
    assert cmp_len == 2 * CMP_STRIDE and 2 * dk == LANES and 2 * rope == LANES and nope == LANES

    n_p = nb * t_p
    n_s = db * t_s
    m = n_p + n_s
    nqw = nsa_heads * dk

    o_nq = q_rank + kv_rank
    o_ta = o_nq + nqw
    o_tb = o_ta + LANES
    o_tc = o_tb + LANES
    o_td = o_tc + LANES
    o_te = o_td + LANES
    tn_g = min(MM_TN, d_model)
    o_ga = -(-(o_te + LANES) // tn_g) * tn_g
    o_gb = o_ga + d_model
    assert 3 * nsa_heads <= LANES and o_gb % tn_g == 0
    widths = [q_rank, kv_rank, rope, nqw] + [dk] * 6 + [3 * nsa_heads, d_model, d_model]
    off = np.cumsum([0] + widths)
    seg = lambda w, i: w[:, int(off[i]):int(off[i + 1])]

    pos_rows = jnp.concatenate([jnp.tile(jnp.arange(t_p, dtype=jnp.int32), nb), jnp.full((n_s,), past_len, jnp.int32)])
    cn, sn = _rope_tables(pos_rows, dk, dk // 4)
    cm, sm = _rope_tables(pos_rows, rope, rope)
    nblk_p = t_p // CMP_STRIDE
    cend_p = jnp.arange(nblk_p, dtype=jnp.int32) * CMP_STRIDE + cmp_len - 1
    cc_p, sc_p = _rope_tables(cend_p, dk, dk // 4)
    lane = jnp.arange(LANES)
    cc_p = jnp.where(lane[None, :] < dk, cc_p, 1.0)
    sc_p = jnp.where(lane[None, :] < dk, sc_p, 0.0)
    cend_s = jnp.arange(past_len // CMP_STRIDE, dtype=jnp.int32) * CMP_STRIDE + cmp_len - 1
    cc_s, sc_s = _rope_tables(cend_s, dk, dk // 4)
    cc_s = jnp.where(lane[None, :] < dk, cc_s, 1.0)
    sc_s = jnp.where(lane[None, :] < dk, sc_s, 0.0)
    pt_flat = page_table.reshape(-1)
    krope_t, kcmp_t, vcmp_t, kslc_t, vslc_t, kwin_t, vwin_t = [
        _token_minor(a) for a in (cache_mla_krope, cache_nsa_kcmp, cache_nsa_vcmp, cache_nsa_kslc, cache_nsa_vslc,
                                  state_nsa_kwin, state_nsa_vwin)]

    gi = jnp.arange(LANES)[:, None]
    hd = jnp.arange(nqw)[None, :] // dk
    expand = jnp.stack([jnp.where(gi == 3 * hd + i, 1.0, 0.0) for i in range(3)]).astype(BF16)

    x = jnp.concatenate([x_prompt.reshape(n_p, d_model), x_sample.reshape(n_s, d_model)], axis=0)
    p_rows, s_rows = [], []
    tk_mla = _pick_tile(t_p, 512, LANES)

    for l in range(depth):
        wl = w_in[l]
        zc = jnp.zeros((d_model, dk), F32)
        w_perm = jnp.concatenate(
            [seg(wl, 0), seg(wl, 1), seg(wl, 3), seg(wl, 6), seg(wl, 8), seg(wl, 2), zc, seg(wl, 4), seg(wl, 5),
             seg(wl, 7), seg(wl, 9), seg(wl, 10), jnp.zeros((d_model, o_ga - o_te - 3 * nsa_heads), F32),
             seg(wl, 11), seg(wl, 12)], axis=1)
        wq = w_qb[l].reshape(q_rank, heads, nope + rope)
        wq_perm = jnp.concatenate([wq[..., :nope].reshape(q_rank, heads * nope),
                                   wq[..., nope:].reshape(q_rank, heads * rope)], axis=1)
        wkb_t = jnp.transpose(w_kb[l], (0, 2, 1)).reshape(heads * nope, kv_rank).astype(BF16)

        h = rms_norm_rows(x, g_attn[l])
        z = matmul(h, w_perm)
        g_a = jnp.concatenate([g_nks[l], g_nkw[l]]).reshape(1, LANES)
        g_b = jnp.concatenate([g_kr[l], jnp.ones((LANES - rope,), F32)]).reshape(1, LANES)
        qa, c, nq, ta, tb = post_in_proj(
            z, g_qa[l].reshape(1, -1), g_kva[l].reshape(1, -1), jnp.tile(g_nq[l], nsa_heads).reshape(1, -1),
            g_a, g_b, cn, sn, cm, sm, q_rank=q_rank, kv_rank=kv_rank, nsa_heads=nsa_heads, dk=dk, mla_rope=rope)
        q = matmul(qa, wq_perm, tn=1024)
        q_abs, q_rope = post_q_proj(
            q, (g_qn[l] * g_kn[l]).reshape(1, -1), jnp.tile(g_qr[l], LANES // rope).reshape(1, -1), w_kb, cm, sm,
            layer=l, heads=heads, nope=nope, rope=rope, kv_rank=kv_rank, scale=mla_scale)

        r_p = prompt_key_scale(c, wkb_t, nb=nb, t=t_p, heads=heads, nope=nope, tk=tk_mla)
        o_lat = prompt_attention(q_abs, c, c, mode="causal", nb=nb, t=t_p, tk=tk_mla, d1=kv_rank, dv=kv_rank,
                                 q2=q_rope, k2=tb, d2=rope, r=r_p)
        cmpw = pack_compress_weights(cmp_pos[l], cmp_w1[l], cmp_b1[l], cmp_w2[l], g_nkc[l])
        kcvc_p = compress_prompt(z, cmpw + (cc_p, sc_p), nb=nb, t=t_p, col_block=o_tc // LANES, dk=dk, hid=hid)
        o_c, sel = prompt_cmp_attention(nq, kcvc_p, nb=nb, t=t_p, dk=dk, cmp_len=cmp_len)
        o_s = prompt_attention(nq, ta, z, mode="select", nb=nb, t=t_p, tk=tk_mla, d1=dk, dv=dk, sel=sel,
                               v_col_block=o_td // LANES)
        o_w = prompt_window_attention(nq, ta, z, nb=nb, t=t_p, d=dk, k_off=dk, v_off=dk, v_col_block=o_td // LANES)

        sl_s = slice(n_p, m)
        c_s = c[sl_s][:, None, :]
        kr_s = tb[sl_s, :rope][:, None, :]
        kcr_s = z[sl_s, o_tc:o_tc + dk][:, None, :]
        vcr_s = z[sl_s, o_tc + dk:o_tc + 2 * dk][:, None, :]
        ks_s = ta[sl_s, :dk][:, None, :]
        kw_s = ta[sl_s, dk:][:, None, :]
        vs_s = z[sl_s, o_td:o_td + dk][:, None, :]
        vw_s = z[sl_s, o_td + dk:o_td + 2 * dk][:, None, :]
        ng_s = z[sl_s, o_te:o_te + 3 * nsa_heads].reshape(n_s, nsa_heads, 3)
        qabs_s = jnp.transpose(q_abs[:, sl_s], (1, 0, 2))
        qrope_s = jnp.transpose(q_rope[:, sl_s], (1, 0, 2))
        nq_s = jnp.transpose(nq[:, sl_s], (1, 0, 2))
        olat_s = sample_mla(pt_flat, cache_mla_latent, krope_t, qabs_s, qrope_s, c_s, tb[sl_s][:, None, :],
                            wkb_t, layer=l, heads=heads, nope=nope)
        oc_s, sel_s = sample_compress_attend(pt_flat, kcmp_t, vcmp_t, nq_s, cmpw + (cc_s, sc_s),
                                             layer=l, cmp_len=cmp_len, hid=hid, past_len=past_len)
        onsa_s = sample_select_window(pt_flat, sel_s.reshape(-1), kslc_t, vslc_t, kwin_t, vwin_t, nq_s,
                                      ta[sl_s][:, None, :], z[sl_s, o_td:o_td + LANES][:, None, :], oc_s, ng_s,
                                      layer=l)
        kw_all = jnp.concatenate([state_nsa_kwin[l], kw_s], axis=1)
        vw_all = jnp.concatenate([state_nsa_vwin[l], vw_s], axis=1)

        o_lat = lax.dynamic_update_slice(o_lat, jnp.transpose(olat_s, (1, 0, 2)), (0, n_p, 0))
        o_mla = head_matmul(o_lat, w_vb, layer=l)
        o_nsa = nsa_gate_combine(o_c, o_s, o_w, z, expand, gate_col_block=o_te // LANES)
        o_nsa = lax.dynamic_update_slice(o_nsa, onsa_s.reshape(n_s, nqw), (n_p, 0))
        y1 = matmul(o_mla, w_mla_o, layer=l, epilogue="gate", extras=(z,), extra_col_blocks=(o_ga // tn_g,), tn=tn_g)
        y = matmul(o_nsa, w_nsa_o, layer=l, epilogue="gate_add", extras=(z, y1), extra_col_blocks=(o_gb // tn_g, 0),
                   out_dtype=BF16, tn=tn_g)
        x = matmul(y, w_out, layer=l, epilogue="add", extras=(x,))
        h2 = rms_norm_rows(x, g_mlp[l])
        u = matmul(h2, w_up, layer=l, epilogue="relu2", out_dtype=BF16)
        x = matmul(u, w_down, layer=l, epilogue="add", extras=(x,), tn=2 * MM_TN, tk=MM_TK // 2)

        pr = lambda a: a[:n_p].reshape(nb, t_p, -1)
        kw_p = pr(ta[:, dk:])
        vw_p = pr(z[:, o_td + dk:o_td + 2 * dk])
        p_rows.append((pr(c), pr(tb[:, :rope]), pr(z[:, o_tc:o_tc + dk]), pr(z[:, o_tc + dk:o_tc + 2 * dk]),
                       pr(ta[:, :dk]), pr(z[:, o_td:o_td + dk]), kw_p[:, t_p - wkeep_p:], vw_p[:, t_p - wkeep_p:]))
        s_rows.append((c_s, kr_s, kcr_s, vcr_s, ks_s, vs_s, kw_all[:, t_s:], vw_all[:, t_s:]))

    xp = x[:n_p].reshape(nb, t_p, d_model)
    xs = x[n_p:].reshape(db, t_s, d_model)
    (lat_p, krope_p, kcmp_p, vcmp_p, kslc_p, vslc_p, kwin_p, vwin_p) = [jnp.stack(a, axis=0) for a in zip(*p_rows)]
    (lat_s, krope_s, kcmp_s, vcmp_s, kslc_s, vslc_s, kwin_s, vwin_s) = [jnp.stack(a, axis=0) for a in zip(*s_rows)]
    return (xp, xs, lat_p, lat_s, krope_p, krope_s, kcmp_p, kcmp_s, vcmp_p, vcmp_s,
            kslc_p, kslc_s, vslc_p, vslc_s, kwin_p, kwin_s, vwin_p, vwin_s)
```
